```python
import jax, jax.numpy as jnp
from jax import lax
import numpy as np

D_MODEL = 2048
BATCH = 4
SEQ = 2048
DEPTH = 4

D_MIX = D_MODEL
HEAD_DIM = 128
ATTN_WIDTH = D_MIX // 2
N_ATTN_HEADS = ATTN_WIDTH // HEAD_DIM
GMLP_WIDTH = D_MIX // 4
N_GMLP_HEADS = 4
GMLP_HEAD_DIM = GMLP_WIDTH // N_GMLP_HEADS
POOL_WIDTH = D_MIX - ATTN_WIDTH - GMLP_WIDTH
N_POOL_GROUPS = 4
POOL_GROUP_DIM = POOL_WIDTH // N_POOL_GROUPS
POOL_WINDOWS = (2, 4, 8, 16)
CHUNK = 128
Q_BLOCK = 128
D_PLE = 256
D_FF = -(-8 * D_MODEL // (3 * 256)) * 256
EPS = 1e-6

PROJ_SIZES = (ATTN_WIDTH, ATTN_WIDTH, ATTN_WIDTH, N_ATTN_HEADS, GMLP_WIDTH, GMLP_WIDTH, POOL_WIDTH)
D_PROJ = 3 * ATTN_WIDTH + N_ATTN_HEADS + 2 * GMLP_WIDTH + POOL_WIDTH
SPLIT_POINTS = (
    ATTN_WIDTH,
    2 * ATTN_WIDTH,
    3 * ATTN_WIDTH,
    3 * ATTN_WIDTH + N_ATTN_HEADS,
    3 * ATTN_WIDTH + N_ATTN_HEADS + GMLP_WIDTH,
    3 * ATTN_WIDTH + N_ATTN_HEADS + 2 * GMLP_WIDTH,
)

kernel_name = "hybrid_parallel_fox_gmlp_pool_block"


def rms_norm(x, gain):
    xf = x.astype(jnp.float32)
    y = xf * lax.rsqrt(jnp.mean(xf * xf, axis=-1, keepdims=True) + EPS)
    return (y * gain.astype(jnp.float32)).astype(x.dtype)


def fox_attention(q, k, v, log_f):
    B, S, H, Dh = q.shape
    nb = S // Q_BLOCK
    c = jnp.cumsum(log_f, axis=1)
    q_blocks = q.reshape(B, nb, Q_BLOCK, H, Dh).transpose(1, 0, 3, 2, 4)
    cq_blocks = c.reshape(B, nb, Q_BLOCK, H).transpose(1, 0, 3, 2)
    pos_blocks = jnp.arange(S, dtype=jnp.int32).reshape(nb, Q_BLOCK)
    kh = k.transpose(0, 2, 1, 3)
    vh = v.transpose(0, 2, 1, 3)
    ck = c.transpose(0, 2, 1)
    kpos = jnp.arange(S, dtype=jnp.int32)
    scale = Dh ** -0.5

    def one_block(args):
        q_blk, cq_blk, qpos = args
        s = jnp.einsum('bhqd,bhkd->bhqk', q_blk, kh, preferred_element_type=jnp.float32) * scale
        s = s + (cq_blk[..., :, None] - ck[:, :, None, :])
        s = jnp.where(qpos[:, None] >= kpos[None, :], s, -jnp.inf)
        w = jax.nn.softmax(s, axis=-1)
        return jnp.einsum('bhqk,bhkd->bhqd', w.astype(vh.dtype), vh)

    out = lax.map(one_block, (q_blocks, cq_blocks, pos_blocks))
    return out.transpose(1, 0, 3, 2, 4).reshape(B, S, H * Dh)


def gmlp_mixer(u, v, v_gain, w_s, b_s):
    B, S, _ = u.shape
    nc = S // CHUNK
    u = jax.nn.gelu(u)
    v = jax.nn.gelu(v).reshape(B, S, N_GMLP_HEADS, GMLP_HEAD_DIM)
    v = rms_norm(v, v_gain).reshape(B, nc, CHUNK, N_GMLP_HEADS, GMLP_HEAD_DIM)
    w = w_s * jnp.tril(jnp.ones((CHUNK, CHUNK), w_s.dtype))[None]
    mixed = jnp.einsum('gts,bnsgc->bntgc', w, v) + b_s.T[None, None, :, :, None]
    return u * mixed.reshape(B, S, GMLP_WIDTH)


def pool_mixer(xp, w_pool, scale):
    B, S, _ = xp.shape
    x4 = xp.reshape(B, S, N_POOL_GROUPS, POOL_GROUP_DIM)
    cs = jnp.cumsum(x4.astype(jnp.float32), axis=1)
    cpad = jnp.concatenate([jnp.zeros((B, 1, N_POOL_GROUPS, POOL_GROUP_DIM), jnp.float32), cs], axis=1)
    t = jnp.arange(S, dtype=jnp.int32)[:, None]
    win = jnp.asarray(POOL_WINDOWS, dtype=jnp.int32)[None, :]
    lo = jnp.maximum(t + 1 - win, 0)
    cnt = (t + 1 - lo).astype(jnp.float32)
    g_idx = jnp.arange(N_POOL_GROUPS, dtype=jnp.int32)[None, :]
    window_sum = cs - cpad[:, lo, g_idx, :]
    d = (window_sum / cnt[None, :, :, None] - x4.astype(jnp.float32)).astype(xp.dtype)
    y = jnp.einsum('bsgc,gcd->bsgd', d, w_pool)
    return y.reshape(B, S, POOL_WIDTH) * scale


def setup_inputs(seed: int = 0) -> dict:
    key = jax.random.key(seed)
    ks = jax.random.split(key, 20)
    f32 = jnp.float32

    def nrm(k, shape, s):
        return jax.random.normal(k, shape, f32) * s

    def gain(k, shape):
        return 1.0 + 0.05 * jax.random.normal(k, shape, f32)

    return {
        "x": jax.random.normal(ks[0], (BATCH, SEQ, D_MODEL), f32),
        "p": jax.random.normal(ks[1], (DEPTH, BATCH, SEQ, D_PLE), f32),
        "norm_mix": gain(ks[2], (DEPTH, D_MODEL)),
        "w_in": nrm(ks[3], (DEPTH, D_MODEL, D_PROJ), D_MODEL ** -0.5),
        "q_norm": gain(ks[4], (DEPTH, HEAD_DIM)),
        "k_norm": gain(ks[5], (DEPTH, HEAD_DIM)),
        "forget_bias": 3.0 + 0.5 * jax.random.normal(ks[6], (DEPTH, N_ATTN_HEADS), f32),
        "gmlp_v_norm": gain(ks[7], (DEPTH, N_GMLP_HEADS, GMLP_HEAD_DIM)),
        "gmlp_w_s": nrm(ks[8], (DEPTH, N_GMLP_HEADS, CHUNK, CHUNK), CHUNK ** -0.5),
        "gmlp_b_s": 1.0 + 0.1 * jax.random.normal(ks[9], (DEPTH, N_GMLP_HEADS, CHUNK), f32),
        "pool_w": nrm(ks[10], (DEPTH, N_POOL_GROUPS, POOL_GROUP_DIM, POOL_GROUP_DIM), POOL_GROUP_DIM ** -0.5),
        "pool_scale": 1.0 + 0.1 * jax.random.normal(ks[11], (DEPTH, POOL_WIDTH), f32),
        "w_out": nrm(ks[12], (DEPTH, D_MIX, D_MODEL), D_MIX ** -0.5),
        "norm_ffn": gain(ks[13], (DEPTH, D_MODEL)),
        "w_ffn_gate": nrm(ks[14], (DEPTH, D_MODEL, D_FF), D_MODEL ** -0.5),
        "w_ffn_up": nrm(ks[15], (DEPTH, D_MODEL, D_FF), D_MODEL ** -0.5),
        "w_ffn_down": nrm(ks[16], (DEPTH, D_FF, D_MODEL), D_FF ** -0.5),
        "norm_ple": gain(ks[17], (DEPTH, D_MODEL)),
        "w_ple_gate": nrm(ks[18], (DEPTH, D_MODEL, D_MODEL), D_MODEL ** -0.5),
        "w_ple_proj": nrm(ks[19], (DEPTH, D_PLE, D_MODEL), D_PLE ** -0.5),
    }


def reference(x, p, norm_mix, w_in, q_norm, k_norm, forget_bias, gmlp_v_norm, gmlp_w_s, gmlp_b_s,
              pool_w, pool_scale, w_out, norm_ffn, w_ffn_gate, w_ffn_up, w_ffn_down,
              norm_ple, w_ple_gate, w_ple_proj):
    B, S, _ = x.shape
    h = x
    for i in range(DEPTH):
        xn = rms_norm(h, norm_mix[i])
        proj = xn @ w_in[i]
        q, k, v, f_logit, gu, gv, xp = jnp.split(proj, SPLIT_POINTS, axis=-1)
        q = rms_norm(q.reshape(B, S, N_ATTN_HEADS, HEAD_DIM), q_norm[i])
        k = rms_norm(k.reshape(B, S, N_ATTN_HEADS, HEAD_DIM), k_norm[i])
        v = v.reshape(B, S, N_ATTN_HEADS, HEAD_DIM)
        log_f = jax.nn.log_sigmoid((f_logit + forget_bias[i]).astype(jnp.float32))
        y_attn = fox_attention(q, k, v, log_f)
        y_gmlp = gmlp_mixer(gu, gv, gmlp_v_norm[i], gmlp_w_s[i], gmlp_b_s[i])
        y_pool = pool_mixer(xp, pool_w[i], pool_scale[i])
        mix = jnp.concatenate([y_attn, y_gmlp, y_pool], axis=-1)
        h = h + mix @ w_out[i]
        xn = rms_norm(h, norm_ffn[i])
        h = h + (jax.nn.silu(xn @ w_ffn_gate[i]) * (xn @ w_ffn_up[i])) @ w_ffn_down[i]
        gate = jax.nn.sigmoid(rms_norm(h, norm_ple[i]) @ w_ple_gate[i])
        h = h + (p[i] @ w_ple_proj[i]) * gate
    return h
```

```python
import functools

import jax
import jax.numpy as jnp
from jax import lax
from jax.experimental import pallas as pl
from jax.experimental.pallas import tpu as pltpu

D_MODEL = 2048
HEAD_DIM = 128
N_ATTN_HEADS = 8
ATTN_WIDTH = N_ATTN_HEADS * HEAD_DIM
N_GMLP_HEADS = 4
GMLP_WIDTH = N_GMLP_HEADS * HEAD_DIM
N_POOL_GROUPS = 4
POOL_WIDTH = N_POOL_GROUPS * HEAD_DIM
POOL_WINDOWS = (2, 4, 8, 16)
CHUNK = 128
D_PLE = 256
EPS = 1e-6
D_MAIN = 3 * ATTN_WIDTH + 2 * GMLP_WIDTH + POOL_WIDTH

LANES = 128
VMEM_BYTES_V7X = 64 * 1024 * 1024

_QB, _KB, _VB = 0, N_ATTN_HEADS, 2 * N_ATTN_HEADS
_GUB = 3 * N_ATTN_HEADS
_GVB = _GUB + N_GMLP_HEADS
_XPB = _GVB + N_GMLP_HEADS

NORM_BM = 512
INPROJ_BM, INPROJ_BN = 1024, 1536
ATTN_TQ = 256
ATTN_TK = 256
CUMSUM_BLOCK = 256
OUT_BM = 512
FFN_BM, FFN_BF = 1024, 512
PLE_BM = 512

F32 = jnp.float32
BF16 = jnp.bfloat16


def _params(semantics, vmem_bytes):
    assert vmem_bytes <= VMEM_BYTES_V7X
    return pltpu.CompilerParams(dimension_semantics=semantics, vmem_limit_bytes=int(vmem_bytes))


def _nbytes(shape, dtype):
    n = 1
    for s in shape:
        n *= s
    return n * jnp.dtype(dtype).itemsize


def _vmem_estimate(pipelined, resident=(), temporaries=()):
    total = 2 * sum(_nbytes(s, d) for s, d in pipelined)
    total += sum(_nbytes(s, d) for s, d in resident)
    total += sum(_nbytes(s, d) for s, d in temporaries)
    return total + 2 * 1024 * 1024


def _rms(x, gain):
    return x * lax.rsqrt(jnp.mean(x * x, axis=-1, keepdims=True) + EPS) * gain


def _norm_kernel(x_ref, g_ref, o_ref):
    o_ref[...] = _rms(x_ref[...], g_ref[...]).astype(o_ref.dtype)


def _norm(x, gain):
    m, d = x.shape
    bm = NORM_BM
    return pl.pallas_call(
        _norm_kernel,
        out_shape=jax.ShapeDtypeStruct((m, d), BF16),
        grid=(m // bm,),
        in_specs=[pl.BlockSpec((bm, d), lambda i: (i, 0)), pl.BlockSpec((1, d), lambda i: (0, 0))],
        out_specs=pl.BlockSpec((bm, d), lambda i: (i, 0)),
        compiler_params=_params(("parallel",), _vmem_estimate(
            [((bm, d), F32), ((bm, d), BF16)], temporaries=[((bm, d), F32)])),
        name="norm0",
    )(x, gain)


def _matmul_kernel(x_ref, w_ref, o_ref):
    o_ref[...] = jnp.dot(x_ref[...], w_ref[...], preferred_element_type=F32)


def _in_proj(xn, w):
    m, d = xn.shape
    n = w.shape[1]
    bm, bn = INPROJ_BM, INPROJ_BN
    return pl.pallas_call(
        _matmul_kernel,
        out_shape=jax.ShapeDtypeStruct((m, n), F32),
        grid=(n // bn, m // bm),
        in_specs=[pl.BlockSpec((bm, d), lambda j, i: (i, 0)), pl.BlockSpec((d, bn), lambda j, i: (0, j))],
        out_specs=pl.BlockSpec((bm, bn), lambda j, i: (i, j)),
        compiler_params=_params(("parallel", "parallel"), _vmem_estimate(
            [((bm, d), BF16), ((d, bn), BF16), ((bm, bn), F32)], temporaries=[((bm, bn), F32)])),
        name="in_proj",
    )(xn, w)


def _split3(x):
    hi = x.astype(BF16)
    r1 = x - hi.astype(F32)
    mid = r1.astype(BF16)
    lo = (r1 - mid.astype(F32)).astype(BF16)
    return hi, mid, lo


def _fox_prep_kernel(xn_ref, wf_ref, fb_ref, c_ref, ct_ref):
    s = xn_ref.shape[0]
    t = CUMSUM_BLOCK
    f = jnp.dot(xn_ref[...], wf_ref[...], preferred_element_type=F32)
    lf = jax.nn.log_sigmoid(f + fb_ref[...])
    row = lax.broadcasted_iota(jnp.int32, (t, t), 0)
    col = lax.broadcasted_iota(jnp.int32, (t, t), 1)
    tri = jnp.where(row >= col, 1.0, 0.0).astype(BF16)
    carry = jnp.zeros((1, LANES), F32)
    for b in range(s // t):
        hi, mid, lo = _split3(lf[b * t:(b + 1) * t, :])
        cb = (jnp.dot(tri, hi, preferred_element_type=F32)
              + jnp.dot(tri, mid, preferred_element_type=F32)
              + jnp.dot(tri, lo, preferred_element_type=F32)) + carry
        c_ref[b * t:(b + 1) * t, :] = cb
        ct_ref[:, b * t:(b + 1) * t] = cb.T[:N_ATTN_HEADS, :]
        carry = cb[t - 1:t, :]


def _fox_prep(xn3, wf, fb):
    bsz, s, d = xn3.shape
    return pl.pallas_call(
        _fox_prep_kernel,
        out_shape=(jax.ShapeDtypeStruct((bsz, s, LANES), F32),
                   jax.ShapeDtypeStruct((bsz, N_ATTN_HEADS, s), F32)),
        grid=(bsz,),
        in_specs=[pl.BlockSpec((None, s, d), lambda b: (b, 0, 0)),
                  pl.BlockSpec((d, LANES), lambda b: (0, 0)),
                  pl.BlockSpec((1, LANES), lambda b: (0, 0))],
        out_specs=(pl.BlockSpec((None, s, LANES), lambda b: (b, 0, 0)),
                   pl.BlockSpec((None, N_ATTN_HEADS, s), lambda b: (b, 0, 0))),
        compiler_params=_params(("parallel",), _vmem_estimate(
            [((s, d), BF16), ((d, LANES), BF16), ((s, LANES), F32), ((N_ATTN_HEADS, s), F32)],
            temporaries=[((s, LANES), F32)] * 4)),
        name="fox_prep",
    )(xn3, wf, fb)


def _attn_kernel(q_ref, k_ref, v_ref, c_ref, ct_ref, qg_ref, kg_ref, o_ref, kn_ref, vb_ref, ck_ref,
                 *, tq, tk):
    h = pl.program_id(1)
    qi = pl.program_id(2)
    nk = k_ref.shape[0] // tk
    scale = HEAD_DIM ** -0.5

    @pl.when(qi == 0)
    def _():
        kn_ref[...] = _rms(k_ref[...], kg_ref[...]).astype(BF16)
        vb_ref[...] = v_ref[...].astype(BF16)
        crow = ct_ref[pl.ds(h, 1), :]
        for j in range(nk):
            ck_ref[j:j + 1, :] = crow[:, j * tk:(j + 1) * tk]

    qn = _rms(q_ref[...], qg_ref[...]).astype(BF16)
    lane = lax.broadcasted_iota(jnp.int32, (tq, LANES), 1)
    cq = jnp.sum(jnp.where(lane == h, c_ref[...], 0.0), axis=1, keepdims=True)

    def step(j, carry, masked):
        m, l, acc = carry
        start = pl.multiple_of(j * tk, tk)
        ks = kn_ref[pl.ds(start, tk), :]
        vs = vb_ref[pl.ds(start, tk), :]
        s = lax.dot_general(qn, ks, (((1,), (1,)), ((), ())), preferred_element_type=F32) * scale
        s = s + (cq - ck_ref[pl.ds(j, 1), :])
        if masked:
            rq = lax.broadcasted_iota(jnp.int32, (tq, tk), 0)
            rk = lax.broadcasted_iota(jnp.int32, (tq, tk), 1)
            s = jnp.where(rq >= rk, s, -jnp.inf)
        m_new = jnp.maximum(m, jnp.max(s, axis=1, keepdims=True))
        alpha = jnp.exp(m - m_new)
        p = jnp.exp(s - m_new)
        l_new = alpha * l + jnp.sum(p, axis=1, keepdims=True)
        acc_new = alpha * acc + jnp.dot(p.astype(BF16), vs, preferred_element_type=F32)
        return m_new, l_new, acc_new

    init = (jnp.full((tq, 1), -jnp.inf, F32), jnp.zeros((tq, 1), F32), jnp.zeros((tq, HEAD_DIM), F32))
    carry = lax.fori_loop(0, qi, lambda j, c: step(j, c, False), init)
    _, l, acc = step(qi, carry, True)
    o_ref[...] = (acc / l).astype(o_ref.dtype)


def _attention(proj3, c, ct, qg, kg):
    bsz, s, _ = proj3.shape
    tq, tk = ATTN_TQ, ATTN_TK
    assert tq == tk
    kernel = functools.partial(_attn_kernel, tq=tq, tk=tk)
    return pl.pallas_call(
        kernel,
        out_shape=jax.ShapeDtypeStruct((bsz, s, ATTN_WIDTH), BF16),
        grid=(bsz, N_ATTN_HEADS, s // tq),
        in_specs=[
            pl.BlockSpec((None, tq, HEAD_DIM), lambda b, h, i: (b, i, _QB + h)),
            pl.BlockSpec((None, s, HEAD_DIM), lambda b, h, i: (b, 0, _KB + h)),
            pl.BlockSpec((None, s, HEAD_DIM), lambda b, h, i: (b, 0, _VB + h)),
            pl.BlockSpec((None, tq, LANES), lambda b, h, i: (b, i, 0)),
            pl.BlockSpec((None, N_ATTN_HEADS, s), lambda b, h, i: (b, 0, 0)),
            pl.BlockSpec((1, HEAD_DIM), lambda b, h, i: (0, 0)),
            pl.BlockSpec((1, HEAD_DIM), lambda b, h, i: (0, 0)),
        ],
        out_specs=pl.BlockSpec((None, tq, HEAD_DIM), lambda b, h, i: (b, i, h)),
        scratch_shapes=[pltpu.VMEM((s, HEAD_DIM), BF16), pltpu.VMEM((s, HEAD_DIM), BF16),
                        pltpu.VMEM((s // tk, tk), F32)],
        compiler_params=_params(("parallel", "parallel", "arbitrary"), _vmem_estimate(
            [((tq, HEAD_DIM), F32), ((s, HEAD_DIM), F32), ((s, HEAD_DIM), F32), ((tq, LANES), F32),
             ((N_ATTN_HEADS, s), F32), ((tq, HEAD_DIM), BF16)],
            resident=[((s, HEAD_DIM), BF16)] * 2 + [((s // tk, tk), F32)],
            temporaries=[((s, HEAD_DIM), F32)] * 2 + [((tq, tk), F32)] * 4)),
        name="fox_attn",
    )(proj3, proj3, proj3, c, ct, qg, kg)


def _mixers_kernel(gu_ref, gv_ref, xp_ref, vg_ref, ws_ref, bs_ref, wp_ref, ps_ref, yg_ref, yp_ref):
    g = pl.program_id(1)
    s = gu_ref.shape[0]

    vn = _rms(jax.nn.gelu(gv_ref[...]), vg_ref[...]).astype(BF16)
    row = lax.broadcasted_iota(jnp.int32, (CHUNK, CHUNK), 0)
    col = lax.broadcasted_iota(jnp.int32, (CHUNK, CHUNK), 1)
    w = jnp.where(row >= col, ws_ref[...], 0.0).astype(BF16)
    bias = bs_ref[...]
    for n in range(s // CHUNK):
        sl = slice(n * CHUNK, (n + 1) * CHUNK)
        mixed = jnp.dot(w, vn[sl, :], preferred_element_type=F32) + bias
        yg_ref[sl, :] = (jax.nn.gelu(gu_ref[sl, :]) * mixed).astype(yg_ref.dtype)

    x = xp_ref[...]
    t = lax.broadcasted_iota(jnp.int32, x.shape, 0)
    ws = x
    acc = x
    for level in range(len(POOL_WINDOWS)):
        shift = 1 << level
        acc = acc + jnp.where(t >= shift, pltpu.roll(acc, shift, 0), 0.0)
        ws = jnp.where(g == level, acc, ws)
    window = jnp.left_shift(2, g)
    cnt = jnp.minimum(t + 1, window).astype(F32)
    d = (ws / cnt - x).astype(BF16)
    y = jnp.dot(d, wp_ref[...], preferred_element_type=F32) * ps_ref[...]
    yp_ref[...] = y.astype(yp_ref.dtype)


def _mixers(proj3, v_gain, w_s, b_rep, w_pool, p_scale):
    bsz, s, _ = proj3.shape
    col = lambda base: (lambda b, g: (b, 0, base + g))
    per_group = lambda b, g: (g, 0, 0)
    blk = (None, s, HEAD_DIM)
    return pl.pallas_call(
        _mixers_kernel,
        out_shape=(jax.ShapeDtypeStruct((bsz, s, GMLP_WIDTH), BF16),
                   jax.ShapeDtypeStruct((bsz, s, POOL_WIDTH), BF16)),
        grid=(bsz, N_GMLP_HEADS),
        in_specs=[
            pl.BlockSpec(blk, col(_GUB)), pl.BlockSpec(blk, col(_GVB)), pl.BlockSpec(blk, col(_XPB)),
            pl.BlockSpec((None, 1, HEAD_DIM), per_group),
            pl.BlockSpec((None, CHUNK, CHUNK), per_group),
            pl.BlockSpec((None, CHUNK, HEAD_DIM), per_group),
            pl.BlockSpec((None, HEAD_DIM, HEAD_DIM), per_group),
            pl.BlockSpec((None, 1, HEAD_DIM), per_group),
        ],
        out_specs=(pl.BlockSpec(blk, lambda b, g: (b, 0, g)), pl.BlockSpec(blk, lambda b, g: (b, 0, g))),
        compiler_params=_params(("parallel", "parallel"), _vmem_estimate(
            [((s, HEAD_DIM), F32)] * 3 + [((s, HEAD_DIM), BF16)] * 2 + [((CHUNK, CHUNK), F32)] * 3,
            temporaries=[((s, HEAD_DIM), F32)] * 8)),
        name="mixers",
    )(proj3, proj3, proj3, v_gain, w_s, b_rep, w_pool, p_scale)


def _out_proj_kernel(ya_ref, yg_ref, yp_ref, h_ref, w_ref, g_ref, h1_ref, xn_ref):
    a0, a1, a2 = ATTN_WIDTH, ATTN_WIDTH + GMLP_WIDTH, D_MODEL
    acc = jnp.dot(ya_ref[...], w_ref[0:a0, :], preferred_element_type=F32)
    acc += jnp.dot(yg_ref[...], w_ref[a0:a1, :], preferred_element_type=F32)
    acc += jnp.dot(yp_ref[...], w_ref[a1:a2, :], preferred_element_type=F32)
    h1 = h_ref[...] + acc
    h1_ref[...] = h1
    xn_ref[...] = _rms(h1, g_ref[...]).astype(xn_ref.dtype)


def _out_proj(ya, yg, yp, h, w, gain):
    m, d = h.shape
    bm = OUT_BM
    row = lambda i: (i, 0)
    fixed = lambda i: (0, 0)
    return pl.pallas_call(
        _out_proj_kernel,
        out_shape=(jax.ShapeDtypeStruct((m, d), F32), jax.ShapeDtypeStruct((m, d), BF16)),
        grid=(m // bm,),
        in_specs=[pl.BlockSpec((bm, ATTN_WIDTH), row), pl.BlockSpec((bm, GMLP_WIDTH), row),
                  pl.BlockSpec((bm, POOL_WIDTH), row), pl.BlockSpec((bm, d), row),
                  pl.BlockSpec((d, d), fixed), pl.BlockSpec((1, d), fixed)],
        out_specs=(pl.BlockSpec((bm, d), row), pl.BlockSpec((bm, d), row)),
        compiler_params=_params(("parallel",), _vmem_estimate(
            [((bm, d), BF16), ((bm, d), F32), ((d, d), BF16), ((bm, d), F32), ((bm, d), BF16)],
            temporaries=[((bm, d), F32)] * 2)),
        name="out_proj",
    )(ya, yg, yp, h, w, gain)


def _ffn_kernel(xn_ref, wg_ref, wu_ref, wd_ref, o_ref):
    j = pl.program_id(1)
    x = xn_ref[...]
    gate = jnp.dot(x, wg_ref[...], preferred_element_type=F32)
    up = jnp.dot(x, wu_ref[...], preferred_element_type=F32)
    a = (jax.nn.silu(gate) * up).astype(BF16)
    part = jnp.dot(a, wd_ref[...], preferred_element_type=F32)

    @pl.when(j == 0)
    def _():
        o_ref[...] = part

    @pl.when(j != 0)
    def _():
        o_ref[...] += part


def _ffn(xn, wg, wu, wd):
    m, d = xn.shape
    ff = wg.shape[1]
    bm, bf = FFN_BM, FFN_BF
    return pl.pallas_call(
        _ffn_kernel,
        out_shape=jax.ShapeDtypeStruct((m, d), F32),
        grid=(m // bm, ff // bf),
        in_specs=[pl.BlockSpec((bm, d), lambda i, j: (i, 0)),
                  pl.BlockSpec((d, bf), lambda i, j: (0, j)),
                  pl.BlockSpec((d, bf), lambda i, j: (0, j)),
                  pl.BlockSpec((bf, d), lambda i, j: (j, 0))],
        out_specs=pl.BlockSpec((bm, d), lambda i, j: (i, 0)),
        compiler_params=_params(("parallel", "arbitrary"), _vmem_estimate(
            [((bm, d), BF16), ((d, bf), BF16), ((d, bf), BF16), ((bf, d), BF16), ((bm, d), F32)],
            temporaries=[((bm, bf), F32)] * 3 + [((bm, d), F32)])),
        name="ffn",
    )(xn, wg, wu, wd)


def _ple_kernel(h1_ref, f_ref, p_ref, wg_ref, wp_ref, gp_ref, gn_ref, h3_ref, *xn_ref):
    h2 = h1_ref[...] + f_ref[...]
    xn = _rms(h2, gp_ref[...]).astype(BF16)
    gate = jax.nn.sigmoid(jnp.dot(xn, wg_ref[...], preferred_element_type=F32))
    pe = jnp.dot(p_ref[...].astype(BF16), wp_ref[...], preferred_element_type=F32)
    h3 = h2 + pe * gate
    h3_ref[...] = h3
    if xn_ref:
        xn_ref[0][...] = _rms(h3, gn_ref[...]).astype(BF16)


def _ple(h1, f, p, wg, wp, gain_ple, gain_next, with_next):
    m, d = h1.shape
    bm = PLE_BM
    row = lambda i: (i, 0)
    fixed = lambda i: (0, 0)
    out_shape = [jax.ShapeDtypeStruct((m, d), F32)]
    out_specs = [pl.BlockSpec((bm, d), row)]
    blocks = [((bm, d), F32)] * 3 + [((bm, D_PLE), F32), ((d, d), BF16), ((D_PLE, d), BF16)]
    if with_next:
        out_shape.append(jax.ShapeDtypeStruct((m, d), BF16))
        out_specs.append(pl.BlockSpec((bm, d), row))
        blocks.append(((bm, d), BF16))
    return pl.pallas_call(
        _ple_kernel,
        out_shape=tuple(out_shape),
        grid=(m // bm,),
        in_specs=[pl.BlockSpec((bm, d), row), pl.BlockSpec((bm, d), row), pl.BlockSpec((bm, D_PLE), row),
                  pl.BlockSpec((d, d), fixed), pl.BlockSpec((D_PLE, d), fixed),
                  pl.BlockSpec((1, d), fixed), pl.BlockSpec((1, d), fixed)],
        out_specs=tuple(out_specs),
        compiler_params=_params(("parallel",), _vmem_estimate(
            blocks, temporaries=[((bm, d), F32)] * 3)),
        name="ple",
    )(h1, f, p, wg, wp, gain_ple, gain_next)


def kernel(x, p, norm_mix, w_in, q_norm, k_norm, forget_bias, gmlp_v_norm, gmlp_w_s, gmlp_b_s,
           pool_w, pool_scale, w_out, norm_ffn, w_ffn_gate, w_ffn_up, w_ffn_down,
           norm_ple, w_ple_gate, w_ple_proj):
    bsz, s, d = x.shape
    depth = w_in.shape[0]
    m = bsz * s
    f_lo = 3 * ATTN_WIDTH
    f_hi = f_lo + N_ATTN_HEADS

    h = x.reshape(m, d)
    xn = _norm(h, norm_mix[0][None, :])
    for i in range(depth):
        w_main = jnp.concatenate([w_in[i][:, :f_lo], w_in[i][:, f_hi:]], axis=1).astype(BF16)
        w_f = jnp.pad(w_in[i][:, f_lo:f_hi], ((0, 0), (0, LANES - N_ATTN_HEADS))).astype(BF16)
        f_bias = jnp.pad(forget_bias[i], (0, LANES - N_ATTN_HEADS))[None, :]
        b_rep = jnp.broadcast_to(gmlp_b_s[i][:, :, None], (N_GMLP_HEADS, CHUNK, HEAD_DIM))

        proj = _in_proj(xn, w_main)
        proj3 = proj.reshape(bsz, s, D_MAIN)
        c, ct = _fox_prep(xn.reshape(bsz, s, d), w_f, f_bias)
        ya = _attention(proj3, c, ct, q_norm[i][None, :], k_norm[i][None, :])
        yg, yp = _mixers(proj3, gmlp_v_norm[i][:, None, :], gmlp_w_s[i], b_rep, pool_w[i].astype(BF16),
                         pool_scale[i].reshape(N_POOL_GROUPS, 1, HEAD_DIM))
        h1, xn_ffn = _out_proj(ya.reshape(m, ATTN_WIDTH), yg.reshape(m, GMLP_WIDTH),
                               yp.reshape(m, POOL_WIDTH), h, w_out[i].astype(BF16), norm_ffn[i][None, :])
        f = _ffn(xn_ffn, w_ffn_gate[i].astype(BF16), w_ffn_up[i].astype(BF16), w_ffn_down[i].astype(BF16))
        with_next = i + 1 < depth
        gain_next = norm_mix[i + 1] if with_next else norm_mix[i]
        outs = _ple(h1, f, p[i].reshape(m, D_PLE), w_ple_gate[i].astype(BF16), w_ple_proj[i].astype(BF16),
                    norm_ple[i][None, :], gain_next[None, :], with_next)
        h = outs[0]
        if with_next:
            xn = outs[1]
    return h.reshape(bsz, s, d)
```

```python
import functools

import jax
import jax.numpy as jnp
from jax import lax
from jax.experimental import pallas as pl
from jax.experimental.pallas import tpu as pltpu

D_MODEL = 2048
HEAD_DIM = 128
N_ATTN_HEADS = 8
ATTN_WIDTH = N_ATTN_HEADS * HEAD_DIM
N_GMLP_HEADS = 4
GMLP_WIDTH = N_GMLP_HEADS * HEAD_DIM
N_POOL_GROUPS = 4
POOL_WIDTH = N_POOL_GROUPS * HEAD_DIM
POOL_WINDOWS = (2, 4, 8, 16)
CHUNK = 128
D_PLE = 256
EPS = 1e-6
D_MAIN = 3 * ATTN_WIDTH + 2 * GMLP_WIDTH + POOL_WIDTH

LANES = 128
VMEM_BYTES_V7X = 64 * 1024 * 1024

_QB, _KB, _VB = 0, N_ATTN_HEADS, 2 * N_ATTN_HEADS
_GUB = 3 * N_ATTN_HEADS
_GVB = _GUB + N_GMLP_HEADS
_XPB = _GVB + N_GMLP_HEADS

NORM_BM = 512
INPROJ_BM, INPROJ_BN = 1024, 1536
ATTN_TQ = 256
CUMSUM_BLOCK = 256
OUT_BM = 512
FFN_BM, FFN_BF = 1024, 512
PLE_BM = 512

F32 = jnp.float32
BF16 = jnp.bfloat16


def _params(semantics, vmem_bytes):
    assert vmem_bytes <= VMEM_BYTES_V7X
    return pltpu.CompilerParams(dimension_semantics=semantics, vmem_limit_bytes=int(vmem_bytes))


def _nbytes(shape, dtype):
    n = 1
    for s in shape:
        n *= s
    return n * jnp.dtype(dtype).itemsize


def _vmem_estimate(pipelined, resident=(), temporaries=()):
    total = 2 * sum(_nbytes(s, d) for s, d in pipelined)
    total += sum(_nbytes(s, d) for s, d in resident)
    total += sum(_nbytes(s, d) for s, d in temporaries)
    return total + 2 * 1024 * 1024


def _rms(x, gain):
    return x * lax.rsqrt(jnp.mean(x * x, axis=-1, keepdims=True) + EPS) * gain


def _layer_row(layer, width):
    return pl.BlockSpec((None, 1, width), lambda i: (layer, 0, 0))


def _norm_kernel(x_ref, g_ref, o_ref):
    o_ref[...] = _rms(x_ref[...], g_ref[...]).astype(o_ref.dtype)


def _norm(x, gains, layer):
    m, d = x.shape
    bm = NORM_BM
    return pl.pallas_call(
        _norm_kernel,
        out_shape=jax.ShapeDtypeStruct((m, d), BF16),
        grid=(m // bm,),
        in_specs=[pl.BlockSpec((bm, d), lambda i: (i, 0)), _layer_row(layer, d)],
        out_specs=pl.BlockSpec((bm, d), lambda i: (i, 0)),
        compiler_params=_params(("parallel",), _vmem_estimate(
            [((bm, d), F32), ((bm, d), BF16)], temporaries=[((bm, d), F32)])),
        name="norm0",
    )(x, gains)


def _matmul_kernel(x_ref, w_ref, o_ref):
    o_ref[...] = jnp.dot(x_ref[...], w_ref[...], preferred_element_type=F32)


def _in_proj(xn, w, layer):
    m, d = xn.shape
    n = w.shape[2]
    bm, bn = INPROJ_BM, INPROJ_BN
    return pl.pallas_call(
        _matmul_kernel,
        out_shape=jax.ShapeDtypeStruct((m, n), F32),
        grid=(n // bn, m // bm),
        in_specs=[pl.BlockSpec((bm, d), lambda j, i: (i, 0)),
                  pl.BlockSpec((None, d, bn), lambda j, i: (layer, 0, j))],
        out_specs=pl.BlockSpec((bm, bn), lambda j, i: (i, j)),
        compiler_params=_params(("parallel", "parallel"), _vmem_estimate(
            [((bm, d), BF16), ((d, bn), BF16), ((bm, bn), F32)], temporaries=[((bm, bn), F32)])),
        name="in_proj",
    )(xn, w)


def _split3(x):
    hi = x.astype(BF16)
    r1 = x - hi.astype(F32)
    mid = r1.astype(BF16)
    lo = (r1 - mid.astype(F32)).astype(BF16)
    return hi, mid, lo


def _fox_prep_kernel(xn_ref, wf_ref, fb_ref, c_ref, ct_ref):
    s = xn_ref.shape[0]
    t = CUMSUM_BLOCK
    f = jnp.dot(xn_ref[...], wf_ref[...], preferred_element_type=F32)
    lf = jax.nn.log_sigmoid(f + fb_ref[...])
    row = lax.broadcasted_iota(jnp.int32, (t, t), 0)
    col = lax.broadcasted_iota(jnp.int32, (t, t), 1)
    tri = jnp.where(row >= col, 1.0, 0.0).astype(BF16)
    carry = jnp.zeros((1, LANES), F32)
    for b in range(s // t):
        hi, mid, lo = _split3(lf[b * t:(b + 1) * t, :])
        cb = (jnp.dot(tri, hi, preferred_element_type=F32)
              + jnp.dot(tri, mid, preferred_element_type=F32)
              + jnp.dot(tri, lo, preferred_element_type=F32)) + carry
        c_ref[b * t:(b + 1) * t, :] = cb
        ct_ref[:, b * t:(b + 1) * t] = cb.T[:N_ATTN_HEADS, :]
        carry = cb[t - 1:t, :]


def _fox_prep(xn3, wf, fb, layer):
    bsz, s, d = xn3.shape
    return pl.pallas_call(
        _fox_prep_kernel,
        out_shape=(jax.ShapeDtypeStruct((bsz, s, LANES), F32),
                   jax.ShapeDtypeStruct((bsz, N_ATTN_HEADS, s), F32)),
        grid=(bsz,),
        in_specs=[pl.BlockSpec((None, s, d), lambda b: (b, 0, 0)),
                  pl.BlockSpec((None, d, LANES), lambda b: (layer, 0, 0)),
                  _layer_row(layer, LANES)],
        out_specs=(pl.BlockSpec((None, s, LANES), lambda b: (b, 0, 0)),
                   pl.BlockSpec((None, N_ATTN_HEADS, s), lambda b: (b, 0, 0))),
        compiler_params=_params(("parallel",), _vmem_estimate(
            [((s, d), BF16), ((d, LANES), BF16), ((s, LANES), F32), ((N_ATTN_HEADS, s), F32)],
            temporaries=[((s, LANES), F32)] * 4)),
        name="fox_prep",
    )(xn3, wf, fb)


def _dot_nt(a, b):
    return lax.dot_general(a, b, (((1,), (1,)), ((), ())), preferred_element_type=F32)


def _attn_kernel(q_ref, k_ref, v_ref, c_ref, ct_ref, qg_ref, kg_ref, o_ref, kn_ref, vb_ref, *, tq):
    h = pl.program_id(1)
    s_len = q_ref.shape[0]
    scale = HEAD_DIM ** -0.5
    kn_ref[...] = _rms(k_ref[...], kg_ref[...]).astype(BF16)
    vb_ref[...] = v_ref[...].astype(BF16)
    ck = ct_ref[pl.ds(h, 1), :]
    lane = lax.broadcasted_iota(jnp.int32, (tq, LANES), 1)
    causal = (lax.broadcasted_iota(jnp.int32, (tq, tq), 0) >= lax.broadcasted_iota(jnp.int32, (tq, tq), 1))
    for i in range(s_len // tq):
        r0, r1 = i * tq, (i + 1) * tq
        qn = _rms(q_ref[r0:r1, :], qg_ref[...]).astype(BF16)
        cq = jnp.sum(jnp.where(lane == h, c_ref[r0:r1, :], 0.0), axis=1, keepdims=True)
        sd = _dot_nt(qn, kn_ref[r0:r1, :]) * scale + (cq - ck[:, r0:r1])
        sd = jnp.where(causal, sd, -jnp.inf)
        m = jnp.max(sd, axis=1, keepdims=True)
        if i > 0:
            so = _dot_nt(qn, kn_ref[0:r0, :]) * scale + (cq - ck[:, 0:r0])
            m = jnp.maximum(m, jnp.max(so, axis=1, keepdims=True))
        pd = jnp.exp(sd - m)
        l = jnp.sum(pd, axis=1, keepdims=True)
        acc = jnp.dot(pd.astype(BF16), vb_ref[r0:r1, :], preferred_element_type=F32)
        if i > 0:
            po = jnp.exp(so - m)
            l = l + jnp.sum(po, axis=1, keepdims=True)
            acc = acc + jnp.dot(po.astype(BF16), vb_ref[0:r0, :], preferred_element_type=F32)
        o_ref[r0:r1, :] = (acc / l).astype(o_ref.dtype)


def _attention(proj3, c, ct, qg, kg, layer):
    bsz, s, _ = proj3.shape
    tq = ATTN_TQ
    kernel = functools.partial(_attn_kernel, tq=tq)
    blk = (None, s, HEAD_DIM)
    gain = pl.BlockSpec((None, 1, HEAD_DIM), lambda b, h: (layer, 0, 0))
    return pl.pallas_call(
        kernel,
        out_shape=jax.ShapeDtypeStruct((bsz, s, ATTN_WIDTH), BF16),
        grid=(bsz, N_ATTN_HEADS),
        in_specs=[
            pl.BlockSpec(blk, lambda b, h: (b, 0, _QB + h)),
            pl.BlockSpec(blk, lambda b, h: (b, 0, _KB + h)),
            pl.BlockSpec(blk, lambda b, h: (b, 0, _VB + h)),
            pl.BlockSpec((None, s, LANES), lambda b, h: (b, 0, 0)),
            pl.BlockSpec((None, N_ATTN_HEADS, s), lambda b, h: (b, 0, 0)),
            gain, gain,
        ],
        out_specs=pl.BlockSpec(blk, lambda b, h: (b, 0, h)),
        scratch_shapes=[pltpu.VMEM((s, HEAD_DIM), BF16), pltpu.VMEM((s, HEAD_DIM), BF16)],
        compiler_params=_params(("parallel", "parallel"), _vmem_estimate(
            [((s, HEAD_DIM), F32)] * 4 + [((N_ATTN_HEADS, s), F32), ((s, HEAD_DIM), BF16)],
            resident=[((s, HEAD_DIM), BF16)] * 2,
            temporaries=[((s, HEAD_DIM), F32)] * 2 + [((tq, s), F32)] * 4)),
        name="fox_attn",
    )(proj3, proj3, proj3, c, ct, qg, kg)


def _mixers_kernel(gu_ref, gv_ref, xp_ref, vg_ref, ws_ref, bs_ref, wp_ref, ps_ref, yg_ref, yp_ref):
    g = pl.program_id(1)
    s = gu_ref.shape[0]

    vn = _rms(jax.nn.gelu(gv_ref[...]), vg_ref[...]).astype(BF16)
    row = lax.broadcasted_iota(jnp.int32, (CHUNK, CHUNK), 0)
    col = lax.broadcasted_iota(jnp.int32, (CHUNK, CHUNK), 1)
    w = jnp.where(row >= col, ws_ref[...], 0.0).astype(BF16)
    bias = bs_ref[...]
    for n in range(s // CHUNK):
        sl = slice(n * CHUNK, (n + 1) * CHUNK)
        mixed = jnp.dot(w, vn[sl, :], preferred_element_type=F32) + bias
        yg_ref[sl, :] = (jax.nn.gelu(gu_ref[sl, :]) * mixed).astype(yg_ref.dtype)

    x = xp_ref[...]
    t = lax.broadcasted_iota(jnp.int32, x.shape, 0)
    ws = x
    acc = x
    for level in range(len(POOL_WINDOWS)):
        shift = 1 << level
        acc = acc + jnp.where(t >= shift, pltpu.roll(acc, shift, 0), 0.0)
        ws = jnp.where(g == level, acc, ws)
    window = jnp.left_shift(2, g)
    cnt = jnp.minimum(t + 1, window).astype(F32)
    d = (ws / cnt - x).astype(BF16)
    y = jnp.dot(d, wp_ref[...], preferred_element_type=F32) * ps_ref[...]
    yp_ref[...] = y.astype(yp_ref.dtype)


def _mixers(proj3, v_gain, w_s, b_rep, w_pool, p_scale, layer):
    bsz, s, _ = proj3.shape
    col = lambda base: (lambda b, g: (b, 0, base + g))
    per_group = lambda b, g: (layer, g, 0, 0)
    blk = (None, s, HEAD_DIM)
    return pl.pallas_call(
        _mixers_kernel,
        out_shape=(jax.ShapeDtypeStruct((bsz, s, GMLP_WIDTH), BF16),
                   jax.ShapeDtypeStruct((bsz, s, POOL_WIDTH), BF16)),
        grid=(bsz, N_GMLP_HEADS),
        in_specs=[
            pl.BlockSpec(blk, col(_GUB)), pl.BlockSpec(blk, col(_GVB)), pl.BlockSpec(blk, col(_XPB)),
            pl.BlockSpec((None, None, 1, HEAD_DIM), per_group),
            pl.BlockSpec((None, None, CHUNK, CHUNK), per_group),
            pl.BlockSpec((None, None, CHUNK, HEAD_DIM), per_group),
            pl.BlockSpec((None, None, HEAD_DIM, HEAD_DIM), per_group),
            pl.BlockSpec((None, None, 1, HEAD_DIM), per_group),
        ],
        out_specs=(pl.BlockSpec(blk, lambda b, g: (b, 0, g)), pl.BlockSpec(blk, lambda b, g: (b, 0, g))),
        compiler_params=_params(("parallel", "parallel"), _vmem_estimate(
            [((s, HEAD_DIM), F32)] * 3 + [((s, HEAD_DIM), BF16)] * 2 + [((CHUNK, CHUNK), F32)] * 3,
            temporaries=[((s, HEAD_DIM), F32)] * 8)),
        name="mixers",
    )(proj3, proj3, proj3, v_gain, w_s, b_rep, w_pool, p_scale)


def _out_proj_kernel(ya_ref, yg_ref, yp_ref, h_ref, w_ref, g_ref, h1_ref, xn_ref):
    a0, a1, a2 = ATTN_WIDTH, ATTN_WIDTH + GMLP_WIDTH, D_MODEL
    acc = jnp.dot(ya_ref[...], w_ref[0:a0, :], preferred_element_type=F32)
    acc += jnp.dot(yg_ref[...], w_ref[a0:a1, :], preferred_element_type=F32)
    acc += jnp.dot(yp_ref[...], w_ref[a1:a2, :], preferred_element_type=F32)
    h1 = h_ref[...] + acc
    h1_ref[...] = h1
    xn_ref[...] = _rms(h1, g_ref[...]).astype(xn_ref.dtype)


def _out_proj(ya, yg, yp, h, w, gains, layer):
    m, d = h.shape
    bm = OUT_BM
    row = lambda i: (i, 0)
    return pl.pallas_call(
        _out_proj_kernel,
        out_shape=(jax.ShapeDtypeStruct((m, d), F32), jax.ShapeDtypeStruct((m, d), BF16)),
        grid=(m // bm,),
        in_specs=[pl.BlockSpec((bm, ATTN_WIDTH), row), pl.BlockSpec((bm, GMLP_WIDTH), row),
                  pl.BlockSpec((bm, POOL_WIDTH), row), pl.BlockSpec((bm, d), row),
                  pl.BlockSpec((None, d, d), lambda i: (layer, 0, 0)), _layer_row(layer, d)],
        out_specs=(pl.BlockSpec((bm, d), row), pl.BlockSpec((bm, d), row)),
        compiler_params=_params(("parallel",), _vmem_estimate(
            [((bm, d), BF16), ((bm, d), F32), ((d, d), BF16), ((bm, d), F32), ((bm, d), BF16)],
            temporaries=[((bm, d), F32)] * 2)),
        name="out_proj",
    )(ya, yg, yp, h, w, gains)


def _ffn_kernel(xn_ref, wg_ref, wu_ref, wd_ref, o_ref):
    j = pl.program_id(1)
    x = xn_ref[...]
    gate = jnp.dot(x, wg_ref[...], preferred_element_type=F32)
    up = jnp.dot(x, wu_ref[...], preferred_element_type=F32)
    a = (jax.nn.silu(gate) * up).astype(BF16)
    part = jnp.dot(a, wd_ref[...], preferred_element_type=F32)

    @pl.when(j == 0)
    def _():
        o_ref[...] = part

    @pl.when(j != 0)
    def _():
        o_ref[...] += part


def _ffn(xn, wg, wu, wd, layer):
    m, d = xn.shape
    ff = wg.shape[2]
    bm, bf = FFN_BM, FFN_BF
    return pl.pallas_call(
        _ffn_kernel,
        out_shape=jax.ShapeDtypeStruct((m, d), F32),
        grid=(m // bm, ff // bf),
        in_specs=[pl.BlockSpec((bm, d), lambda i, j: (i, 0)),
                  pl.BlockSpec((None, d, bf), lambda i, j: (layer, 0, j)),
                  pl.BlockSpec((None, d, bf), lambda i, j: (layer, 0, j)),
                  pl.BlockSpec((None, bf, d), lambda i, j: (layer, j, 0))],
        out_specs=pl.BlockSpec((bm, d), lambda i, j: (i, 0)),
        compiler_params=_params(("parallel", "arbitrary"), _vmem_estimate(
            [((bm, d), BF16), ((d, bf), BF16), ((d, bf), BF16), ((bf, d), BF16), ((bm, d), F32)],
            temporaries=[((bm, bf), F32)] * 3 + [((bm, d), F32)])),
        name="ffn",
    )(xn, wg, wu, wd)


def _ple_kernel(h1_ref, f_ref, p_ref, wg_ref, wp_ref, gp_ref, *rest):
    h2 = h1_ref[...] + f_ref[...]
    xn = _rms(h2, gp_ref[...]).astype(BF16)
    gate = jax.nn.sigmoid(jnp.dot(xn, wg_ref[...], preferred_element_type=F32))
    pe = jnp.dot(p_ref[...].astype(BF16), wp_ref[...], preferred_element_type=F32)
    h3 = h2 + pe * gate
    if len(rest) == 1:
        (h3_ref,) = rest
    else:
        gn_ref, h3_ref, xn_ref = rest
        xn_ref[...] = _rms(h3, gn_ref[...]).astype(BF16)
    h3_ref[...] = h3


def _ple(h1, f, p, wg, wp, gains_ple, gains_mix, layer, with_next):
    m, d = h1.shape
    bm = PLE_BM
    row = lambda i: (i, 0)
    in_specs = [pl.BlockSpec((bm, d), row), pl.BlockSpec((bm, d), row),
                pl.BlockSpec((None, bm, D_PLE), lambda i: (layer, i, 0)),
                pl.BlockSpec((None, d, d), lambda i: (layer, 0, 0)),
                pl.BlockSpec((None, D_PLE, d), lambda i: (layer, 0, 0)),
                _layer_row(layer, d)]
    args = [h1, f, p, wg, wp, gains_ple]
    out_shape = [jax.ShapeDtypeStruct((m, d), F32)]
    out_specs = [pl.BlockSpec((bm, d), row)]
    blocks = [((bm, d), F32)] * 3 + [((bm, D_PLE), F32), ((d, d), BF16), ((D_PLE, d), BF16)]
    if with_next:
        in_specs.append(_layer_row(layer + 1, d))
        args.append(gains_mix)
        out_shape.append(jax.ShapeDtypeStruct((m, d), BF16))
        out_specs.append(pl.BlockSpec((bm, d), row))
        blocks.append(((bm, d), BF16))
    return pl.pallas_call(
        _ple_kernel,
        out_shape=tuple(out_shape),
        grid=(m // bm,),
        in_specs=in_specs,
        out_specs=tuple(out_specs),
        compiler_params=_params(("parallel",), _vmem_estimate(
            blocks, temporaries=[((bm, d), F32)] * 3)),
        name="ple",
    )(*args)


def kernel(x, p, norm_mix, w_in, q_norm, k_norm, forget_bias, gmlp_v_norm, gmlp_w_s, gmlp_b_s,
           pool_w, pool_scale, w_out, norm_ffn, w_ffn_gate, w_ffn_up, w_ffn_down,
           norm_ple, w_ple_gate, w_ple_proj):
    bsz, s, d = x.shape
    depth = w_in.shape[0]
    m = bsz * s
    f_lo = 3 * ATTN_WIDTH
    f_hi = f_lo + N_ATTN_HEADS

    w_main = jnp.concatenate([w_in[:, :, :f_lo], w_in[:, :, f_hi:]], axis=2).astype(BF16)
    w_f = jnp.pad(w_in[:, :, f_lo:f_hi], ((0, 0), (0, 0), (0, LANES - N_ATTN_HEADS))).astype(BF16)
    f_bias = jnp.pad(forget_bias, ((0, 0), (0, LANES - N_ATTN_HEADS)))[:, None, :]
    b_rep = jnp.broadcast_to(gmlp_b_s[..., None], (depth, N_GMLP_HEADS, CHUNK, HEAD_DIM))
    v_gain = gmlp_v_norm[:, :, None, :]
    p_scale = pool_scale.reshape(depth, N_POOL_GROUPS, 1, HEAD_DIM)
    pool_w16 = pool_w.astype(BF16)
    w_out16 = w_out.astype(BF16)
    w_g16, w_u16, w_d16 = w_ffn_gate.astype(BF16), w_ffn_up.astype(BF16), w_ffn_down.astype(BF16)
    w_pg16, w_pp16 = w_ple_gate.astype(BF16), w_ple_proj.astype(BF16)
    g_mix, g_ffn, g_ple = norm_mix[:, None, :], norm_ffn[:, None, :], norm_ple[:, None, :]
    g_q, g_k = q_norm[:, None, :], k_norm[:, None, :]
    p2 = p.reshape(depth, m, D_PLE)

    h = x.reshape(m, d)
    xn = _norm(h, g_mix, 0)
    for i in range(depth):
        proj3 = _in_proj(xn, w_main, i).reshape(bsz, s, D_MAIN)
        c, ct = _fox_prep(xn.reshape(bsz, s, d), w_f, f_bias, i)
        ya = _attention(proj3, c, ct, g_q, g_k, i)
        yg, yp = _mixers(proj3, v_gain, gmlp_w_s, b_rep, pool_w16, p_scale, i)
        h1, xn_ffn = _out_proj(ya.reshape(m, ATTN_WIDTH), yg.reshape(m, GMLP_WIDTH),
                               yp.reshape(m, POOL_WIDTH), h, w_out16, g_ffn, i)
        f = _ffn(xn_ffn, w_g16, w_u16, w_d16, i)
        with_next = i + 1 < depth
        outs = _ple(h1, f, p2, w_pg16, w_pp16, g_ple, g_mix, i, with_next)
        h = outs[0]
        if with_next:
            xn = outs[1]
    return h.reshape(bsz, s, d)
```

```python
import functools

import jax
import jax.numpy as jnp
from jax import lax
from jax.experimental import pallas as pl
from jax.experimental.pallas import tpu as pltpu

D_MODEL = 2048
HEAD_DIM = 128
N_ATTN_HEADS = 8
ATTN_WIDTH = N_ATTN_HEADS * HEAD_DIM
N_GMLP_HEADS = 4
GMLP_WIDTH = N_GMLP_HEADS * HEAD_DIM
N_POOL_GROUPS = 4
POOL_WIDTH = N_POOL_GROUPS * HEAD_DIM
POOL_WINDOWS = (2, 4, 8, 16)
CHUNK = 128
D_PLE = 256
EPS = 1e-6
D_MAIN = 3 * ATTN_WIDTH + 2 * GMLP_WIDTH + POOL_WIDTH

LANES = 128
VMEM_BYTES_V7X = 64 * 1024 * 1024

_QB, _KB, _VB = 0, N_ATTN_HEADS, 2 * N_ATTN_HEADS
_GUB = 3 * N_ATTN_HEADS
_GVB = _GUB + N_GMLP_HEADS
_XPB = _GVB + N_GMLP_HEADS

NORM_BM = 512
INPROJ_BM, INPROJ_BN = 1024, 1536
ATTN_TQ = 256
CUMSUM_BLOCK = 256
OUT_BM = 512
FFN_BM, FFN_BF = 1024, 512
PLE_BM = 512

F32 = jnp.float32
BF16 = jnp.bfloat16


def _params(semantics, vmem_bytes):
    assert vmem_bytes <= VMEM_BYTES_V7X
    return pltpu.CompilerParams(dimension_semantics=semantics, vmem_limit_bytes=int(vmem_bytes))


def _nbytes(shape, dtype):
    n = 1
    for s in shape:
        n *= s
    return n * jnp.dtype(dtype).itemsize


def _vmem_estimate(pipelined, resident=(), temporaries=()):
    total = 2 * sum(_nbytes(s, d) for s, d in pipelined)
    total += sum(_nbytes(s, d) for s, d in resident)
    total += sum(_nbytes(s, d) for s, d in temporaries)
    return total + 2 * 1024 * 1024


def _rms(x, gain):
    return x * lax.rsqrt(jnp.mean(x * x, axis=-1, keepdims=True) + EPS) * gain


def _layer_row(layer, width):
    return pl.BlockSpec((None, 1, width), lambda i: (layer, 0, 0))


def _norm_kernel(x_ref, g_ref, o_ref):
    o_ref[...] = _rms(x_ref[...], g_ref[...]).astype(o_ref.dtype)


def _norm(x, gains, layer):
    m, d = x.shape
    bm = NORM_BM
    return pl.pallas_call(
        _norm_kernel,
        out_shape=jax.ShapeDtypeStruct((m, d), BF16),
        grid=(m // bm,),
        in_specs=[pl.BlockSpec((bm, d), lambda i: (i, 0)), _layer_row(layer, d)],
        out_specs=pl.BlockSpec((bm, d), lambda i: (i, 0)),
        compiler_params=_params(("parallel",), _vmem_estimate(
            [((bm, d), F32), ((bm, d), BF16)], temporaries=[((bm, d), F32)])),
        name="norm0",
    )(x, gains)


def _matmul_kernel(x_ref, w_ref, o_ref):
    o_ref[...] = jnp.dot(x_ref[...], w_ref[...], preferred_element_type=F32)


def _in_proj(xn, w, layer):
    m, d = xn.shape
    n = w.shape[2]
    bm, bn = INPROJ_BM, INPROJ_BN
    return pl.pallas_call(
        _matmul_kernel,
        out_shape=jax.ShapeDtypeStruct((m, n), F32),
        grid=(n // bn, m // bm),
        in_specs=[pl.BlockSpec((bm, d), lambda j, i: (i, 0)),
                  pl.BlockSpec((None, d, bn), lambda j, i: (layer, 0, j))],
        out_specs=pl.BlockSpec((bm, bn), lambda j, i: (i, j)),
        compiler_params=_params(("parallel", "parallel"), _vmem_estimate(
            [((bm, d), BF16), ((d, bn), BF16), ((bm, bn), F32)], temporaries=[((bm, bn), F32)])),
        name="in_proj",
    )(xn, w)


def _split3(x):
    hi = x.astype(BF16)
    r1 = x - hi.astype(F32)
    mid = r1.astype(BF16)
    lo = (r1 - mid.astype(F32)).astype(BF16)
    return hi, mid, lo


_BIAS_LANES = LANES // N_ATTN_HEADS


def _bias_selector():
    r = lax.broadcasted_iota(jnp.int32, (3 * LANES, 2 * LANES), 0)
    c = lax.broadcasted_iota(jnp.int32, (3 * LANES, 2 * LANES), 1)
    part, head = r // LANES, r % LANES
    target = jnp.where(c < LANES, _BIAS_LANES * head + part, LANES + _BIAS_LANES * head + 3 + part)
    return jnp.where((head < N_ATTN_HEADS) & (c == target), 1.0, 0.0).astype(BF16)


def _fox_prep_kernel(xn_ref, wf_ref, fb_ref, gq_ref, gk_ref):
    s = xn_ref.shape[0]
    t = CUMSUM_BLOCK
    inv_scale = HEAD_DIM ** 0.5
    f = jnp.dot(xn_ref[...], wf_ref[...], preferred_element_type=F32)
    lf = jax.nn.log_sigmoid(f + fb_ref[...])
    row = lax.broadcasted_iota(jnp.int32, (t, t), 0)
    col = lax.broadcasted_iota(jnp.int32, (t, t), 1)
    tri = jnp.where(row >= col, 1.0, 0.0).astype(BF16)
    sel = _bias_selector()
    slot = lax.broadcasted_iota(jnp.int32, (1, LANES), 1) % _BIAS_LANES
    ones_q = jnp.where((slot >= 3) & (slot < 6), 1.0, 0.0)
    ones_k = jnp.where(slot < 3, 1.0, 0.0)
    carry = jnp.zeros((1, LANES), F32)
    for b in range(s // t):
        hi, mid, lo = _split3(lf[b * t:(b + 1) * t, :])
        cb = (jnp.dot(tri, hi, preferred_element_type=F32)
              + jnp.dot(tri, mid, preferred_element_type=F32)
              + jnp.dot(tri, lo, preferred_element_type=F32)) + carry
        carry = cb[t - 1:t, :]
        parts = jnp.concatenate(_split3(cb * inv_scale), axis=1)
        placed = jnp.dot(parts, sel, preferred_element_type=F32)
        gq_ref[b * t:(b + 1) * t, :] = (placed[:, :LANES] + ones_q).astype(BF16)
        gk_ref[b * t:(b + 1) * t, :] = (ones_k - placed[:, LANES:]).astype(BF16)


def _fox_prep(xn3, wf, fb, layer):
    bsz, s, d = xn3.shape
    out = jax.ShapeDtypeStruct((bsz, s, LANES), BF16)
    return pl.pallas_call(
        _fox_prep_kernel,
        out_shape=(out, out),
        grid=(bsz,),
        in_specs=[pl.BlockSpec((None, s, d), lambda b: (b, 0, 0)),
                  pl.BlockSpec((None, d, LANES), lambda b: (layer, 0, 0)),
                  _layer_row(layer, LANES)],
        out_specs=(pl.BlockSpec((None, s, LANES), lambda b: (b, 0, 0)),
                   pl.BlockSpec((None, s, LANES), lambda b: (b, 0, 0))),
        compiler_params=_params(("parallel",), _vmem_estimate(
            [((s, d), BF16), ((d, LANES), BF16), ((s, LANES), BF16), ((s, LANES), BF16)],
            temporaries=[((s, LANES), F32)] * 4)),
        name="fox_prep",
    )(xn3, wf, fb)


_LOG2E = 1.4426950408889634


def _dot_nt(a, b):
    return lax.dot_general(a, b, (((1,), (1,)), ((), ())), preferred_element_type=F32)


def _attn_kernel(q_ref, k_ref, v_ref, gq_ref, gk_ref, qg_ref, kg_ref, o_ref, ka_ref, vb_ref, *, tq):
    h = pl.program_id(1)
    s_len = q_ref.shape[0]
    scale = HEAD_DIM ** -0.5
    ka_ref[:, :HEAD_DIM] = _rms(k_ref[...], kg_ref[...]).astype(BF16)
    ka_ref[:, HEAD_DIM:] = gk_ref[...]
    vb_ref[...] = v_ref[...].astype(BF16)
    lane = lax.broadcasted_iota(jnp.int32, (tq, LANES), 1)
    own_lanes = (lane >= _BIAS_LANES * h) & (lane < _BIAS_LANES * (h + 1))
    causal = (lax.broadcasted_iota(jnp.int32, (tq, tq), 0) >= lax.broadcasted_iota(jnp.int32, (tq, tq), 1))
    for i in range(s_len // tq):
        r0, r1 = i * tq, (i + 1) * tq
        qn = _rms(q_ref[r0:r1, :], qg_ref[...]).astype(BF16)
        gq = jnp.where(own_lanes, gq_ref[r0:r1, :], jnp.zeros((), BF16))
        qa = jnp.concatenate([qn, gq], axis=1)
        sd = jnp.where(causal, _dot_nt(qa, ka_ref[r0:r1, :]), -jnp.inf)
        m = jnp.max(sd, axis=1, keepdims=True)
        if i > 0:
            so = _dot_nt(qa, ka_ref[0:r0, :])
            m = jnp.maximum(m, jnp.max(so, axis=1, keepdims=True))
        pd = jnp.exp2((sd - m) * (scale * _LOG2E))
        l = jnp.sum(pd, axis=1, keepdims=True)
        acc = jnp.dot(pd.astype(BF16), vb_ref[r0:r1, :], preferred_element_type=F32)
        if i > 0:
            po = jnp.exp2((so - m) * (scale * _LOG2E))
            l = l + jnp.sum(po, axis=1, keepdims=True)
            acc = acc + jnp.dot(po.astype(BF16), vb_ref[0:r0, :], preferred_element_type=F32)
        o_ref[r0:r1, :] = (acc / l).astype(o_ref.dtype)


def _attention(proj3, gq, gk, qg, kg, layer):
    bsz, s, _ = proj3.shape
    tq = ATTN_TQ
    kernel = functools.partial(_attn_kernel, tq=tq)
    blk = (None, s, HEAD_DIM)
    gain = pl.BlockSpec((None, 1, HEAD_DIM), lambda b, h: (layer, 0, 0))
    bias = pl.BlockSpec((None, s, LANES), lambda b, h: (b, 0, 0))
    scratch = [((s, HEAD_DIM + LANES), BF16), ((s, HEAD_DIM), BF16)]
    return pl.pallas_call(
        kernel,
        out_shape=jax.ShapeDtypeStruct((bsz, s, ATTN_WIDTH), BF16),
        grid=(bsz, N_ATTN_HEADS),
        in_specs=[
            pl.BlockSpec(blk, lambda b, h: (b, 0, _QB + h)),
            pl.BlockSpec(blk, lambda b, h: (b, 0, _KB + h)),
            pl.BlockSpec(blk, lambda b, h: (b, 0, _VB + h)),
            bias, bias, gain, gain,
        ],
        out_specs=pl.BlockSpec(blk, lambda b, h: (b, 0, h)),
        scratch_shapes=[pltpu.VMEM(shape, dtype) for shape, dtype in scratch],
        compiler_params=_params(("parallel", "parallel"), _vmem_estimate(
            [((s, HEAD_DIM), F32)] * 3 + [((s, HEAD_DIM), BF16)] * 3,
            resident=scratch,
            temporaries=[((s, HEAD_DIM), F32)] * 4 + [((tq, s), F32)] * 4)),
        name="fox_attn",
    )(proj3, proj3, proj3, gq, gk, qg, kg)


def _mixers_kernel(gu_ref, gv_ref, xp_ref, vg_ref, ws_ref, bs_ref, wp_ref, ps_ref, yg_ref, yp_ref):
    g = pl.program_id(1)
    s = gu_ref.shape[0]

    vn = _rms(jax.nn.gelu(gv_ref[...]), vg_ref[...]).astype(BF16)
    row = lax.broadcasted_iota(jnp.int32, (CHUNK, CHUNK), 0)
    col = lax.broadcasted_iota(jnp.int32, (CHUNK, CHUNK), 1)
    w = jnp.where(row >= col, ws_ref[...], 0.0).astype(BF16)
    bias = bs_ref[...]
    for n in range(s // CHUNK):
        sl = slice(n * CHUNK, (n + 1) * CHUNK)
        mixed = jnp.dot(w, vn[sl, :], preferred_element_type=F32) + bias
        yg_ref[sl, :] = (jax.nn.gelu(gu_ref[sl, :]) * mixed).astype(yg_ref.dtype)

    x = xp_ref[...]
    t = lax.broadcasted_iota(jnp.int32, x.shape, 0)
    ws = x
    acc = x
    for level in range(len(POOL_WINDOWS)):
        shift = 1 << level
        acc = acc + jnp.where(t >= shift, pltpu.roll(acc, shift, 0), 0.0)
        ws = jnp.where(g == level, acc, ws)
    window = jnp.left_shift(2, g)
    cnt = jnp.minimum(t + 1, window).astype(F32)
    d = (ws / cnt - x).astype(BF16)
    y = jnp.dot(d, wp_ref[...], preferred_element_type=F32) * ps_ref[...]
    yp_ref[...] = y.astype(yp_ref.dtype)


def _mixers(proj3, v_gain, w_s, b_rep, w_pool, p_scale, layer):
    bsz, s, _ = proj3.shape
    col = lambda base: (lambda b, g: (b, 0, base + g))
    per_group = lambda b, g: (layer, g, 0, 0)
    blk = (None, s, HEAD_DIM)
    return pl.pallas_call(
        _mixers_kernel,
        out_shape=(jax.ShapeDtypeStruct((bsz, s, GMLP_WIDTH), BF16),
                   jax.ShapeDtypeStruct((bsz, s, POOL_WIDTH), BF16)),
        grid=(bsz, N_GMLP_HEADS),
        in_specs=[
            pl.BlockSpec(blk, col(_GUB)), pl.BlockSpec(blk, col(_GVB)), pl.BlockSpec(blk, col(_XPB)),
            pl.BlockSpec((None, None, 1, HEAD_DIM), per_group),
            pl.BlockSpec((None, None, CHUNK, CHUNK), per_group),
            pl.BlockSpec((None, None, CHUNK, HEAD_DIM), per_group),
            pl.BlockSpec((None, None, HEAD_DIM, HEAD_DIM), per_group),
            pl.BlockSpec((None, None, 1, HEAD_DIM), per_group),
        ],
        out_specs=(pl.BlockSpec(blk, lambda b, g: (b, 0, g)), pl.BlockSpec(blk, lambda b, g: (b, 0, g))),
        compiler_params=_params(("parallel", "parallel"), _vmem_estimate(
            [((s, HEAD_DIM), F32)] * 3 + [((s, HEAD_DIM), BF16)] * 2 + [((CHUNK, CHUNK), F32)] * 3,
            temporaries=[((s, HEAD_DIM), F32)] * 8)),
        name="mixers",
    )(proj3, proj3, proj3, v_gain, w_s, b_rep, w_pool, p_scale)


def _out_proj_kernel(ya_ref, yg_ref, yp_ref, h_ref, w_ref, g_ref, h1_ref, xn_ref):
    a0, a1, a2 = ATTN_WIDTH, ATTN_WIDTH + GMLP_WIDTH, D_MODEL
    acc = jnp.dot(ya_ref[...], w_ref[0:a0, :], preferred_element_type=F32)
    acc += jnp.dot(yg_ref[...], w_ref[a0:a1, :], preferred_element_type=F32)
    acc += jnp.dot(yp_ref[...], w_ref[a1:a2, :], preferred_element_type=F32)
    h1 = h_ref[...] + acc
    h1_ref[...] = h1
    xn_ref[...] = _rms(h1, g_ref[...]).astype(xn_ref.dtype)


def _out_proj(ya, yg, yp, h, w, gains, layer):
    m, d = h.shape
    bm = OUT_BM
    row = lambda i: (i, 0)
    return pl.pallas_call(
        _out_proj_kernel,
        out_shape=(jax.ShapeDtypeStruct((m, d), F32), jax.ShapeDtypeStruct((m, d), BF16)),
        grid=(m // bm,),
        in_specs=[pl.BlockSpec((bm, ATTN_WIDTH), row), pl.BlockSpec((bm, GMLP_WIDTH), row),
                  pl.BlockSpec((bm, POOL_WIDTH), row), pl.BlockSpec((bm, d), row),
                  pl.BlockSpec((None, d, d), lambda i: (layer, 0, 0)), _layer_row(layer, d)],
        out_specs=(pl.BlockSpec((bm, d), row), pl.BlockSpec((bm, d), row)),
        compiler_params=_params(("parallel",), _vmem_estimate(
            [((bm, d), BF16), ((bm, d), F32), ((d, d), BF16), ((bm, d), F32), ((bm, d), BF16)],
            temporaries=[((bm, d), F32)] * 2)),
        name="out_proj",
    )(ya, yg, yp, h, w, gains)


def _ffn_kernel(xn_ref, wg_ref, wu_ref, wd_ref, o_ref):
    @pl.when(pl.program_id(1) == 0)
    def _():
        o_ref[...] = jnp.zeros_like(o_ref)

    x = xn_ref[...]
    gate = jnp.dot(x, wg_ref[...], preferred_element_type=F32)
    up = jnp.dot(x, wu_ref[...], preferred_element_type=F32)
    a = (jax.nn.silu(gate) * up).astype(BF16)
    o_ref[...] += jnp.dot(a, wd_ref[...], preferred_element_type=F32)


def _ffn(xn, wg, wu, wd, layer):
    m, d = xn.shape
    ff = wg.shape[2]
    bm, bf = FFN_BM, FFN_BF
    return pl.pallas_call(
        _ffn_kernel,
        out_shape=jax.ShapeDtypeStruct((m, d), F32),
        grid=(m // bm, ff // bf),
        in_specs=[pl.BlockSpec((bm, d), lambda i, j: (i, 0)),
                  pl.BlockSpec((None, d, bf), lambda i, j: (layer, 0, j)),
                  pl.BlockSpec((None, d, bf), lambda i, j: (layer, 0, j)),
                  pl.BlockSpec((None, bf, d), lambda i, j: (layer, j, 0))],
        out_specs=pl.BlockSpec((bm, d), lambda i, j: (i, 0)),
        compiler_params=_params(("parallel", "arbitrary"), _vmem_estimate(
            [((bm, d), BF16), ((d, bf), BF16), ((d, bf), BF16), ((bf, d), BF16), ((bm, d), F32)],
            temporaries=[((bm, bf), F32)] * 3 + [((bm, d), F32)])),
        name="ffn",
    )(xn, wg, wu, wd)


def _ple_kernel(h1_ref, f_ref, p_ref, wg_ref, wp_ref, gp_ref, *rest):
    h2 = h1_ref[...] + f_ref[...]
    xn = _rms(h2, gp_ref[...]).astype(BF16)
    gate = jax.nn.sigmoid(jnp.dot(xn, wg_ref[...], preferred_element_type=F32))
    pe = jnp.dot(p_ref[...].astype(BF16), wp_ref[...], preferred_element_type=F32)
    h3 = h2 + pe * gate
    if len(rest) == 1:
        (h3_ref,) = rest
    else:
        gn_ref, h3_ref, xn_ref = rest
        xn_ref[...] = _rms(h3, gn_ref[...]).astype(BF16)
    h3_ref[...] = h3


def _ple(h1, f, p, wg, wp, gains_ple, gains_mix, layer, with_next):
    m, d = h1.shape
    bm = PLE_BM
    row = lambda i: (i, 0)
    in_specs = [pl.BlockSpec((bm, d), row), pl.BlockSpec((bm, d), row),
                pl.BlockSpec((None, bm, D_PLE), lambda i: (layer, i, 0)),
                pl.BlockSpec((None, d, d), lambda i: (layer, 0, 0)),
                pl.BlockSpec((None, D_PLE, d), lambda i: (layer, 0, 0)),
                _layer_row(layer, d)]
    args = [h1, f, p, wg, wp, gains_ple]
    out_shape = [jax.ShapeDtypeStruct((m, d), F32)]
    out_specs = [pl.BlockSpec((bm, d), row)]
    blocks = [((bm, d), F32)] * 3 + [((bm, D_PLE), F32), ((d, d), BF16), ((D_PLE, d), BF16)]
    if with_next:
        in_specs.append(_layer_row(layer + 1, d))
        args.append(gains_mix)
        out_shape.append(jax.ShapeDtypeStruct((m, d), BF16))
        out_specs.append(pl.BlockSpec((bm, d), row))
        blocks.append(((bm, d), BF16))
    return pl.pallas_call(
        _ple_kernel,
        out_shape=tuple(out_shape),
        grid=(m // bm,),
        in_specs=in_specs,
        out_specs=tuple(out_specs),
        compiler_params=_params(("parallel",), _vmem_estimate(
            blocks, temporaries=[((bm, d), F32)] * 3)),
        name="ple",
    )(*args)


def kernel(x, p, norm_mix, w_in, q_norm, k_norm, forget_bias, gmlp_v_norm, gmlp_w_s, gmlp_b_s,
           pool_w, pool_scale, w_out, norm_ffn, w_ffn_gate, w_ffn_up, w_ffn_down,
           norm_ple, w_ple_gate, w_ple_proj):
    bsz, s, d = x.shape
    depth = w_in.shape[0]
    m = bsz * s
    f_lo = 3 * ATTN_WIDTH
    f_hi = f_lo + N_ATTN_HEADS

    w_main = jnp.concatenate([w_in[:, :, :f_lo], w_in[:, :, f_hi:]], axis=2).astype(BF16)
    w_f = jnp.pad(w_in[:, :, f_lo:f_hi], ((0, 0), (0, 0), (0, LANES - N_ATTN_HEADS))).astype(BF16)
    f_bias = jnp.pad(forget_bias, ((0, 0), (0, LANES - N_ATTN_HEADS)))[:, None, :]
    b_rep = jnp.broadcast_to(gmlp_b_s[..., None], (depth, N_GMLP_HEADS, CHUNK, HEAD_DIM))
    v_gain = gmlp_v_norm[:, :, None, :]
    p_scale = pool_scale.reshape(depth, N_POOL_GROUPS, 1, HEAD_DIM)
    pool_w16 = pool_w.astype(BF16)
    w_out16 = w_out.astype(BF16)
    w_g16, w_u16, w_d16 = w_ffn_gate.astype(BF16), w_ffn_up.astype(BF16), w_ffn_down.astype(BF16)
    w_pg16, w_pp16 = w_ple_gate.astype(BF16), w_ple_proj.astype(BF16)
    g_mix, g_ffn, g_ple = norm_mix[:, None, :], norm_ffn[:, None, :], norm_ple[:, None, :]
    g_q, g_k = q_norm[:, None, :], k_norm[:, None, :]
    p2 = p.reshape(depth, m, D_PLE)

    h = x.reshape(m, d)
    xn = _norm(h, g_mix, 0)
    for i in range(depth):
        proj3 = _in_proj(xn, w_main, i).reshape(bsz, s, D_MAIN)
        gq, gk = _fox_prep(xn.reshape(bsz, s, d), w_f, f_bias, i)
        ya = _attention(proj3, gq, gk, g_q, g_k, i)
        yg, yp = _mixers(proj3, v_gain, gmlp_w_s, b_rep, pool_w16, p_scale, i)
        h1, xn_ffn = _out_proj(ya.reshape(m, ATTN_WIDTH), yg.reshape(m, GMLP_WIDTH),
                               yp.reshape(m, POOL_WIDTH), h, w_out16, g_ffn, i)
        f = _ffn(xn_ffn, w_g16, w_u16, w_d16, i)
        with_next = i + 1 < depth
        outs = _ple(h1, f, p2, w_pg16, w_pp16, g_ple, g_mix, i, with_next)
        h = outs[0]
        if with_next:
            xn = outs[1]
    return h.reshape(bsz, s, d)
```

```python
import functools

import jax
import jax.numpy as jnp
from jax import lax
from jax.experimental import pallas as pl
from jax.experimental.pallas import tpu as pltpu

D_MODEL = 2048
HEAD_DIM = 128
N_ATTN_HEADS = 8
ATTN_WIDTH = N_ATTN_HEADS * HEAD_DIM
N_GMLP_HEADS = 4
GMLP_WIDTH = N_GMLP_HEADS * HEAD_DIM
N_POOL_GROUPS = 4
POOL_WIDTH = N_POOL_GROUPS * HEAD_DIM
POOL_WINDOWS = (2, 4, 8, 16)
CHUNK = 128
D_PLE = 256
EPS = 1e-6
D_MAIN = 3 * ATTN_WIDTH + 2 * GMLP_WIDTH + POOL_WIDTH
_F_LO = 3 * ATTN_WIDTH
_F_HI = _F_LO + N_ATTN_HEADS

LANES = 128
VMEM_BYTES_V7X = 64 * 1024 * 1024

_QB, _KB, _VB = 0, N_ATTN_HEADS, 2 * N_ATTN_HEADS
_GUB = 3 * N_ATTN_HEADS
_GVB = _GUB + N_GMLP_HEADS
_XPB = _GVB + N_GMLP_HEADS

NORM_BM = 512
INPROJ_BM, INPROJ_BN = 1024, 1536
ATTN_TQ = 256
CUMSUM_BLOCK = 256
OUT_BM = 512
FFN_BM, FFN_BF = 1024, 512
PLE_BM = 512

F32 = jnp.float32
BF16 = jnp.bfloat16


def _params(semantics, vmem_bytes):
    assert vmem_bytes <= VMEM_BYTES_V7X
    return pltpu.CompilerParams(dimension_semantics=semantics, vmem_limit_bytes=int(vmem_bytes))


def _nbytes(shape, dtype):
    n = 1
    for s in shape:
        n *= s
    return n * jnp.dtype(dtype).itemsize


def _vmem_estimate(pipelined, resident=(), temporaries=()):
    total = 2 * sum(_nbytes(s, d) for s, d in pipelined)
    total += sum(_nbytes(s, d) for s, d in resident)
    total += sum(_nbytes(s, d) for s, d in temporaries)
    return total + 2 * 1024 * 1024


def _rms(x, gain):
    return x * lax.rsqrt(jnp.mean(x * x, axis=-1, keepdims=True) + EPS) * gain


def _layer_row(layer, width):
    return pl.BlockSpec((None, 1, width), lambda i: (layer, 0, 0))


def _norm_kernel(x_ref, g_ref, o_ref):
    o_ref[...] = _rms(x_ref[...], g_ref[...]).astype(o_ref.dtype)


def _norm(x, gains, layer):
    m, d = x.shape
    bm = NORM_BM
    return pl.pallas_call(
        _norm_kernel,
        out_shape=jax.ShapeDtypeStruct((m, d), BF16),
        grid=(m // bm,),
        in_specs=[pl.BlockSpec((bm, d), lambda i: (i, 0)), _layer_row(layer, d)],
        out_specs=pl.BlockSpec((bm, d), lambda i: (i, 0)),
        compiler_params=_params(("parallel",), _vmem_estimate(
            [((bm, d), F32), ((bm, d), BF16)], temporaries=[((bm, d), F32)])),
        name="norm0",
    )(x, gains)


def _matmul_kernel(x_ref, w_ref, o_ref):
    o_ref[...] = jnp.dot(x_ref[...], w_ref[...], preferred_element_type=F32)


def _in_proj(xn, w):
    m, d = xn.shape
    n = w.shape[1]
    bm, bn = INPROJ_BM, INPROJ_BN
    return pl.pallas_call(
        _matmul_kernel,
        out_shape=jax.ShapeDtypeStruct((m, n), F32),
        grid=(n // bn, m // bm),
        in_specs=[pl.BlockSpec((bm, d), lambda j, i: (i, 0)),
                  pl.BlockSpec((d, bn), lambda j, i: (0, j))],
        out_specs=pl.BlockSpec((bm, bn), lambda j, i: (i, j)),
        compiler_params=_params(("parallel", "parallel"), _vmem_estimate(
            [((bm, d), BF16), ((d, bn), BF16), ((bm, bn), F32)], temporaries=[((bm, bn), F32)])),
        name="in_proj",
    )(xn, w)


def _split3(x):
    hi = x.astype(BF16)
    r1 = x - hi.astype(F32)
    mid = r1.astype(BF16)
    lo = (r1 - mid.astype(F32)).astype(BF16)
    return hi, mid, lo


_BIAS_LANES = LANES // N_ATTN_HEADS


def _bias_selector():
    r = lax.broadcasted_iota(jnp.int32, (3 * LANES, 2 * LANES), 0)
    c = lax.broadcasted_iota(jnp.int32, (3 * LANES, 2 * LANES), 1)
    part, head = r // LANES, r % LANES
    target = jnp.where(c < LANES, _BIAS_LANES * head + part, LANES + _BIAS_LANES * head + 3 + part)
    return jnp.where((head < N_ATTN_HEADS) & (c == target), 1.0, 0.0).astype(BF16)


def _fox_prep_kernel(xn_ref, wf_ref, fb_ref, gq_ref, gk_ref):
    s = xn_ref.shape[0]
    t = CUMSUM_BLOCK
    inv_scale = HEAD_DIM ** 0.5
    f = jnp.dot(xn_ref[...], wf_ref[...], preferred_element_type=F32)
    lf = jax.nn.log_sigmoid(f + fb_ref[...])
    row = lax.broadcasted_iota(jnp.int32, (t, t), 0)
    col = lax.broadcasted_iota(jnp.int32, (t, t), 1)
    tri = jnp.where(row >= col, 1.0, 0.0).astype(BF16)
    sel = _bias_selector()
    slot = lax.broadcasted_iota(jnp.int32, (1, LANES), 1) % _BIAS_LANES
    ones_q = jnp.where((slot >= 3) & (slot < 6), 1.0, 0.0)
    ones_k = jnp.where(slot < 3, 1.0, 0.0)
    carry = jnp.zeros((1, LANES), F32)
    for b in range(s // t):
        hi, mid, lo = _split3(lf[b * t:(b + 1) * t, :])
        cb = (jnp.dot(tri, hi, preferred_element_type=F32)
              + jnp.dot(tri, mid, preferred_element_type=F32)
              + jnp.dot(tri, lo, preferred_element_type=F32)) + carry
        carry = cb[t - 1:t, :]
        parts = jnp.concatenate(_split3(cb * inv_scale), axis=1)
        placed = jnp.dot(parts, sel, preferred_element_type=F32)
        gq_ref[b * t:(b + 1) * t, :] = (placed[:, :LANES] + ones_q).astype(BF16)
        gk_ref[b * t:(b + 1) * t, :] = (ones_k - placed[:, LANES:]).astype(BF16)


def _fox_prep(xn3, wf, fb, layer):
    bsz, s, d = xn3.shape
    out = jax.ShapeDtypeStruct((bsz, s, LANES), BF16)
    return pl.pallas_call(
        _fox_prep_kernel,
        out_shape=(out, out),
        grid=(bsz,),
        in_specs=[pl.BlockSpec((None, s, d), lambda b: (b, 0, 0)),
                  pl.BlockSpec((d, LANES), lambda b: (0, 0)),
                  _layer_row(layer, LANES)],
        out_specs=(pl.BlockSpec((None, s, LANES), lambda b: (b, 0, 0)),
                   pl.BlockSpec((None, s, LANES), lambda b: (b, 0, 0))),
        compiler_params=_params(("parallel",), _vmem_estimate(
            [((s, d), BF16), ((d, LANES), BF16), ((s, LANES), BF16), ((s, LANES), BF16)],
            temporaries=[((s, LANES), F32)] * 4)),
        name="fox_prep",
    )(xn3, wf, fb)


_LOG2E = 1.4426950408889634


def _dot_nt(a, b):
    return lax.dot_general(a, b, (((1,), (1,)), ((), ())), preferred_element_type=F32)


def _split_w_in(w):
    main = jnp.concatenate([w[:, :_F_LO], w[:, _F_HI:]], axis=1)
    forget = jnp.concatenate([w[:, _F_LO:_F_HI], jnp.zeros((w.shape[0], LANES - N_ATTN_HEADS), w.dtype)], axis=1)
    return main.astype(BF16), forget.astype(BF16)


def _attn_kernel(q_ref, k_ref, v_ref, gq_ref, gk_ref, qg_ref, kg_ref, *rest, tq, n_cast, with_w_in):
    n_in = n_cast + int(with_w_in)
    n_out = n_cast + 2 * int(with_w_in)
    w32_refs, o_ref, w16_refs = rest[:n_in], rest[n_in], rest[n_in + 1:n_in + 1 + n_out]
    ka_ref, vb_ref = rest[n_in + 1 + n_out:]
    for w32_ref, w16_ref in zip(w32_refs[:n_cast], w16_refs[:n_cast]):
        w16_ref[...] = w32_ref[...].astype(BF16)
    if with_w_in:
        w16_refs[n_cast][...], w16_refs[n_cast + 1][...] = _split_w_in(w32_refs[n_cast][...])

    h = pl.program_id(1)
    s_len = q_ref.shape[0]
    scale = HEAD_DIM ** -0.5
    ka_ref[:, :HEAD_DIM] = _rms(k_ref[...], kg_ref[...]).astype(BF16)
    ka_ref[:, HEAD_DIM:] = gk_ref[...]
    vb_ref[...] = v_ref[...].astype(BF16)
    lane = lax.broadcasted_iota(jnp.int32, (tq, LANES), 1)
    own_lanes = (lane >= _BIAS_LANES * h) & (lane < _BIAS_LANES * (h + 1))
    causal = (lax.broadcasted_iota(jnp.int32, (tq, tq), 0) >= lax.broadcasted_iota(jnp.int32, (tq, tq), 1))
    for i in range(s_len // tq):
        r0, r1 = i * tq, (i + 1) * tq
        qn = _rms(q_ref[r0:r1, :], qg_ref[...]).astype(BF16)
        gq = jnp.where(own_lanes, gq_ref[r0:r1, :], jnp.zeros((), BF16))
        qa = jnp.concatenate([qn, gq], axis=1)
        sd = jnp.where(causal, _dot_nt(qa, ka_ref[r0:r1, :]), -jnp.inf)
        m = jnp.max(sd, axis=1, keepdims=True)
        if i > 0:
            so = _dot_nt(qa, ka_ref[0:r0, :])
            m = jnp.maximum(m, jnp.max(so, axis=1, keepdims=True))
        pd = jnp.exp2((sd - m) * (scale * _LOG2E))
        l = jnp.sum(pd, axis=1, keepdims=True)
        acc = jnp.dot(pd.astype(BF16), vb_ref[r0:r1, :], preferred_element_type=F32)
        if i > 0:
            po = jnp.exp2((so - m) * (scale * _LOG2E))
            l = l + jnp.sum(po, axis=1, keepdims=True)
            acc = acc + jnp.dot(po.astype(BF16), vb_ref[0:r0, :], preferred_element_type=F32)
        o_ref[r0:r1, :] = (acc / l).astype(o_ref.dtype)


def _attention(proj3, gq, gk, qg, kg, layer, weights32, w_in_next):
    bsz, s, _ = proj3.shape
    tq = ATTN_TQ
    steps = bsz * N_ATTN_HEADS
    with_w_in = w_in_next is not None
    kernel = functools.partial(_attn_kernel, tq=tq, n_cast=len(weights32), with_w_in=with_w_in)
    blk = (None, s, HEAD_DIM)
    gain = pl.BlockSpec((None, 1, HEAD_DIM), lambda b, h: (layer, 0, 0))
    bias = pl.BlockSpec((None, s, LANES), lambda b, h: (b, 0, 0))
    scratch = [((s, HEAD_DIM + LANES), BF16), ((s, HEAD_DIM), BF16)]
    step = lambda b, h: b * N_ATTN_HEADS + h
    cast_in, cast_out, cast_shapes, cast_blocks = [], [], [], []

    def add_cast(w, src_layer, out_cols):
        _, rows, cols = w.shape
        slab = rows // steps
        assert slab * steps == rows and slab % 16 == 0, w.shape
        cast_in.append(pl.BlockSpec((None, slab, cols), lambda b, h: (src_layer, step(b, h), 0)))
        cast_blocks.append(((slab, cols), F32))
        for c in out_cols:
            cast_out.append(pl.BlockSpec((slab, c), lambda b, h: (step(b, h), 0)))
            cast_shapes.append(jax.ShapeDtypeStruct((rows, c), BF16))
            cast_blocks.append(((slab, c), BF16))

    for w in weights32:
        add_cast(w, layer, [w.shape[2]])
    if with_w_in:
        add_cast(w_in_next, layer + 1, [D_MAIN, LANES])
    weights32 = (*weights32, w_in_next) if with_w_in else weights32
    outs = pl.pallas_call(
        kernel,
        out_shape=(jax.ShapeDtypeStruct((bsz, s, ATTN_WIDTH), BF16), *cast_shapes),
        grid=(bsz, N_ATTN_HEADS),
        in_specs=[
            pl.BlockSpec(blk, lambda b, h: (b, 0, _QB + h)),
            pl.BlockSpec(blk, lambda b, h: (b, 0, _KB + h)),
            pl.BlockSpec(blk, lambda b, h: (b, 0, _VB + h)),
            bias, bias, gain, gain, *cast_in,
        ],
        out_specs=(pl.BlockSpec(blk, lambda b, h: (b, 0, h)), *cast_out),
        scratch_shapes=[pltpu.VMEM(shape, dtype) for shape, dtype in scratch],
        compiler_params=_params(("parallel", "parallel"), _vmem_estimate(
            [((s, HEAD_DIM), F32)] * 3 + [((s, HEAD_DIM), BF16)] * 3 + cast_blocks,
            resident=scratch,
            temporaries=[((s, HEAD_DIM), F32)] * 4 + [((tq, s), F32)] * 4)),
        name="fox_attn",
    )(proj3, proj3, proj3, gq, gk, qg, kg, *weights32)
    return outs[0], outs[1:]


def _mixers_kernel(gu_ref, gv_ref, xp_ref, vg_ref, ws_ref, bs_ref, wp_ref, ps_ref, yg_ref, yp_ref):
    g = pl.program_id(1)
    s = gu_ref.shape[0]

    vn = _rms(jax.nn.gelu(gv_ref[...]), vg_ref[...]).astype(BF16)
    row = lax.broadcasted_iota(jnp.int32, (CHUNK, CHUNK), 0)
    col = lax.broadcasted_iota(jnp.int32, (CHUNK, CHUNK), 1)
    w = jnp.where(row >= col, ws_ref[...], 0.0).astype(BF16)
    bias = bs_ref[...]
    for n in range(s // CHUNK):
        sl = slice(n * CHUNK, (n + 1) * CHUNK)
        mixed = jnp.dot(w, vn[sl, :], preferred_element_type=F32) + bias
        yg_ref[sl, :] = (jax.nn.gelu(gu_ref[sl, :]) * mixed).astype(yg_ref.dtype)

    x = xp_ref[...]
    t = lax.broadcasted_iota(jnp.int32, x.shape, 0)
    ws = x
    acc = x
    for level in range(len(POOL_WINDOWS)):
        shift = 1 << level
        acc = acc + jnp.where(t >= shift, pltpu.roll(acc, shift, 0), 0.0)
        ws = jnp.where(g == level, acc, ws)
    window = jnp.left_shift(2, g)
    cnt = jnp.minimum(t + 1, window).astype(F32)
    d = (ws / cnt - x).astype(BF16)
    y = jnp.dot(d, wp_ref[...], preferred_element_type=F32) * ps_ref[...]
    yp_ref[...] = y.astype(yp_ref.dtype)


def _mixers(proj3, v_gain, w_s, b_rep, w_pool, p_scale, layer):
    bsz, s, _ = proj3.shape
    col = lambda base: (lambda b, g: (b, 0, base + g))
    per_group = lambda b, g: (layer, g, 0, 0)
    blk = (None, s, HEAD_DIM)
    return pl.pallas_call(
        _mixers_kernel,
        out_shape=(jax.ShapeDtypeStruct((bsz, s, GMLP_WIDTH), BF16),
                   jax.ShapeDtypeStruct((bsz, s, POOL_WIDTH), BF16)),
        grid=(bsz, N_GMLP_HEADS),
        in_specs=[
            pl.BlockSpec(blk, col(_GUB)), pl.BlockSpec(blk, col(_GVB)), pl.BlockSpec(blk, col(_XPB)),
            pl.BlockSpec((None, None, 1, HEAD_DIM), per_group),
            pl.BlockSpec((None, None, CHUNK, CHUNK), per_group),
            pl.BlockSpec((None, None, CHUNK, HEAD_DIM), per_group),
            pl.BlockSpec((None, None, HEAD_DIM, HEAD_DIM), per_group),
            pl.BlockSpec((None, None, 1, HEAD_DIM), per_group),
        ],
        out_specs=(pl.BlockSpec(blk, lambda b, g: (b, 0, g)), pl.BlockSpec(blk, lambda b, g: (b, 0, g))),
        compiler_params=_params(("parallel", "parallel"), _vmem_estimate(
            [((s, HEAD_DIM), F32)] * 3 + [((s, HEAD_DIM), BF16)] * 2 + [((CHUNK, CHUNK), F32)] * 3,
            temporaries=[((s, HEAD_DIM), F32)] * 8)),
        name="mixers",
    )(proj3, proj3, proj3, v_gain, w_s, b_rep, w_pool, p_scale)


def _out_proj_kernel(ya_ref, yg_ref, yp_ref, h_ref, w_ref, g_ref, h1_ref, xn_ref):
    a0, a1, a2 = ATTN_WIDTH, ATTN_WIDTH + GMLP_WIDTH, D_MODEL
    acc = jnp.dot(ya_ref[...], w_ref[0:a0, :], preferred_element_type=F32)
    acc += jnp.dot(yg_ref[...], w_ref[a0:a1, :], preferred_element_type=F32)
    acc += jnp.dot(yp_ref[...], w_ref[a1:a2, :], preferred_element_type=F32)
    h1 = h_ref[...] + acc
    h1_ref[...] = h1
    xn_ref[...] = _rms(h1, g_ref[...]).astype(xn_ref.dtype)


def _out_proj(ya, yg, yp, h, w, gains, layer):
    m, d = h.shape
    bm = OUT_BM
    row = lambda i: (i, 0)
    return pl.pallas_call(
        _out_proj_kernel,
        out_shape=(jax.ShapeDtypeStruct((m, d), F32), jax.ShapeDtypeStruct((m, d), BF16)),
        grid=(m // bm,),
        in_specs=[pl.BlockSpec((bm, ATTN_WIDTH), row), pl.BlockSpec((bm, GMLP_WIDTH), row),
                  pl.BlockSpec((bm, POOL_WIDTH), row), pl.BlockSpec((bm, d), row),
                  pl.BlockSpec((d, d), lambda i: (0, 0)), _layer_row(layer, d)],
        out_specs=(pl.BlockSpec((bm, d), row), pl.BlockSpec((bm, d), row)),
        compiler_params=_params(("parallel",), _vmem_estimate(
            [((bm, d), BF16), ((bm, d), F32), ((d, d), BF16), ((bm, d), F32), ((bm, d), BF16)],
            temporaries=[((bm, d), F32)] * 2)),
        name="out_proj",
    )(ya, yg, yp, h, w, gains)


def _ffn_kernel(xn_ref, wg_ref, wu_ref, wd_ref, o_ref):
    @pl.when(pl.program_id(1) == 0)
    def _():
        o_ref[...] = jnp.zeros_like(o_ref)

    x = xn_ref[...]
    gate = jnp.dot(x, wg_ref[...], preferred_element_type=F32)
    up = jnp.dot(x, wu_ref[...], preferred_element_type=F32)
    a = (jax.nn.silu(gate) * up).astype(BF16)
    o_ref[...] += jnp.dot(a, wd_ref[...], preferred_element_type=F32)


def _ffn(xn, wg, wu, wd):
    m, d = xn.shape
    ff = wg.shape[1]
    bm, bf = FFN_BM, FFN_BF
    return pl.pallas_call(
        _ffn_kernel,
        out_shape=jax.ShapeDtypeStruct((m, d), F32),
        grid=(m // bm, ff // bf),
        in_specs=[pl.BlockSpec((bm, d), lambda i, j: (i, 0)),
                  pl.BlockSpec((d, bf), lambda i, j: (0, j)),
                  pl.BlockSpec((d, bf), lambda i, j: (0, j)),
                  pl.BlockSpec((bf, d), lambda i, j: (j, 0))],
        out_specs=pl.BlockSpec((bm, d), lambda i, j: (i, 0)),
        compiler_params=_params(("parallel", "arbitrary"), _vmem_estimate(
            [((bm, d), BF16), ((d, bf), BF16), ((d, bf), BF16), ((bf, d), BF16), ((bm, d), F32)],
            temporaries=[((bm, bf), F32)] * 3 + [((bm, d), F32)])),
        name="ffn",
    )(xn, wg, wu, wd)


def _ple_kernel(h1_ref, f_ref, p_ref, wg_ref, wp_ref, gp_ref, *rest):
    h2 = h1_ref[...] + f_ref[...]
    xn = _rms(h2, gp_ref[...]).astype(BF16)
    gate = jax.nn.sigmoid(jnp.dot(xn, wg_ref[...], preferred_element_type=F32))
    pe = jnp.dot(p_ref[...].astype(BF16), wp_ref[...], preferred_element_type=F32)
    h3 = h2 + pe * gate
    if len(rest) == 1:
        (h3_ref,) = rest
    else:
        gn_ref, h3_ref, xn_ref = rest
        xn_ref[...] = _rms(h3, gn_ref[...]).astype(BF16)
    h3_ref[...] = h3


def _ple(h1, f, p, wg, wp, gains_ple, gains_mix, layer, with_next):
    m, d = h1.shape
    bm = PLE_BM
    row = lambda i: (i, 0)
    in_specs = [pl.BlockSpec((bm, d), row), pl.BlockSpec((bm, d), row),
                pl.BlockSpec((None, bm, D_PLE), lambda i: (layer, i, 0)),
                pl.BlockSpec((d, d), lambda i: (0, 0)),
                pl.BlockSpec((None, D_PLE, d), lambda i: (layer, 0, 0)),
                _layer_row(layer, d)]
    args = [h1, f, p, wg, wp, gains_ple]
    out_shape = [jax.ShapeDtypeStruct((m, d), F32)]
    out_specs = [pl.BlockSpec((bm, d), row)]
    blocks = [((bm, d), F32)] * 3 + [((bm, D_PLE), F32), ((d, d), BF16), ((D_PLE, d), BF16)]
    if with_next:
        in_specs.append(_layer_row(layer + 1, d))
        args.append(gains_mix)
        out_shape.append(jax.ShapeDtypeStruct((m, d), BF16))
        out_specs.append(pl.BlockSpec((bm, d), row))
        blocks.append(((bm, d), BF16))
    return pl.pallas_call(
        _ple_kernel,
        out_shape=tuple(out_shape),
        grid=(m // bm,),
        in_specs=in_specs,
        out_specs=tuple(out_specs),
        compiler_params=_params(("parallel",), _vmem_estimate(
            blocks, temporaries=[((bm, d), F32)] * 3)),
        name="ple",
    )(*args)


def kernel(x, p, norm_mix, w_in, q_norm, k_norm, forget_bias, gmlp_v_norm, gmlp_w_s, gmlp_b_s,
           pool_w, pool_scale, w_out, norm_ffn, w_ffn_gate, w_ffn_up, w_ffn_down,
           norm_ple, w_ple_gate, w_ple_proj):
    bsz, s, d = x.shape
    depth = w_in.shape[0]
    m = bsz * s
    w0 = w_in[0]
    w_main = jnp.concatenate([w0[:, :_F_LO], w0[:, _F_HI:]], axis=1).astype(BF16)
    w_f = jnp.pad(w0[:, _F_LO:_F_HI], ((0, 0), (0, LANES - N_ATTN_HEADS))).astype(BF16)
    f_bias = jnp.pad(forget_bias, ((0, 0), (0, LANES - N_ATTN_HEADS)))[:, None, :]
    b_rep = jnp.broadcast_to(gmlp_b_s[..., None], (depth, N_GMLP_HEADS, CHUNK, HEAD_DIM))
    v_gain = gmlp_v_norm[:, :, None, :]
    p_scale = pool_scale.reshape(depth, N_POOL_GROUPS, 1, HEAD_DIM)
    pool_w16 = pool_w.astype(BF16)
    w_pp16 = w_ple_proj.astype(BF16)
    later_weights = (w_out, w_ffn_gate, w_ffn_up, w_ffn_down, w_ple_gate)
    g_mix, g_ffn, g_ple = norm_mix[:, None, :], norm_ffn[:, None, :], norm_ple[:, None, :]
    g_q, g_k = q_norm[:, None, :], k_norm[:, None, :]
    p2 = p.reshape(depth, m, D_PLE)

    h = x.reshape(m, d)
    xn = _norm(h, g_mix, 0)
    for i in range(depth):
        with_next = i + 1 < depth
        proj3 = _in_proj(xn, w_main).reshape(bsz, s, D_MAIN)
        gq, gk = _fox_prep(xn.reshape(bsz, s, d), w_f, f_bias, i)
        ya, w16 = _attention(proj3, gq, gk, g_q, g_k, i, later_weights, w_in if with_next else None)
        w_out16, w_g16, w_u16, w_d16, w_pg16 = w16[:5]
        yg, yp = _mixers(proj3, v_gain, gmlp_w_s, b_rep, pool_w16, p_scale, i)
        h1, xn_ffn = _out_proj(ya.reshape(m, ATTN_WIDTH), yg.reshape(m, GMLP_WIDTH),
                               yp.reshape(m, POOL_WIDTH), h, w_out16, g_ffn, i)
        f = _ffn(xn_ffn, w_g16, w_u16, w_d16)
        outs = _ple(h1, f, p2, w_pg16, w_pp16, g_ple, g_mix, i, with_next)
        h = outs[0]
        if with_next:
            xn = outs[1]
            w_main, w_f = w16[5:]
    return h.reshape(bsz, s, d)
```

```python
import functools

import jax
import jax.numpy as jnp
from jax import lax
from jax.experimental import pallas as pl
from jax.experimental.pallas import tpu as pltpu

D_MODEL = 2048
HEAD_DIM = 128
N_ATTN_HEADS = 8
ATTN_WIDTH = N_ATTN_HEADS * HEAD_DIM
N_GMLP_HEADS = 4
GMLP_WIDTH = N_GMLP_HEADS * HEAD_DIM
N_POOL_GROUPS = 4
POOL_WIDTH = N_POOL_GROUPS * HEAD_DIM
POOL_WINDOWS = (2, 4, 8, 16)
CHUNK = 128
D_PLE = 256
EPS = 1e-6
D_MAIN = 3 * ATTN_WIDTH + 2 * GMLP_WIDTH + POOL_WIDTH
_F_LO = 3 * ATTN_WIDTH
_F_HI = _F_LO + N_ATTN_HEADS

LANES = 128
VMEM_BYTES_V7X = 64 * 1024 * 1024

_QB, _KB, _VB = 0, N_ATTN_HEADS, 2 * N_ATTN_HEADS
_GUB = 3 * N_ATTN_HEADS
_GVB = _GUB + N_GMLP_HEADS
_XPB = _GVB + N_GMLP_HEADS

NORM_BM = 512
INPROJ_BM, INPROJ_BN = 1024, 1536
ATTN_TQ = 512
CUMSUM_BLOCK = 256
OUT_BM = 512
FFN_BM, FFN_BF = 1024, 512
PLE_BM = 512

F32 = jnp.float32
BF16 = jnp.bfloat16


def _params(semantics, vmem_bytes):
    assert vmem_bytes <= VMEM_BYTES_V7X
    return pltpu.CompilerParams(dimension_semantics=semantics, vmem_limit_bytes=int(vmem_bytes))


def _nbytes(shape, dtype):
    n = 1
    for s in shape:
        n *= s
    return n * jnp.dtype(dtype).itemsize


def _vmem_estimate(pipelined, resident=(), temporaries=()):
    total = 2 * sum(_nbytes(s, d) for s, d in pipelined)
    total += sum(_nbytes(s, d) for s, d in resident)
    total += sum(_nbytes(s, d) for s, d in temporaries)
    return total + 2 * 1024 * 1024


def _rms(x, gain):
    return x * lax.rsqrt(jnp.mean(x * x, axis=-1, keepdims=True) + EPS) * gain


def _layer_row(layer, width):
    return pl.BlockSpec((None, 1, width), lambda i: (layer, 0, 0))


def _norm_kernel(x_ref, g_ref, o_ref):
    o_ref[...] = _rms(x_ref[...], g_ref[...]).astype(o_ref.dtype)


def _norm(x, gains, layer):
    m, d = x.shape
    bm = NORM_BM
    return pl.pallas_call(
        _norm_kernel,
        out_shape=jax.ShapeDtypeStruct((m, d), BF16),
        grid=(m // bm,),
        in_specs=[pl.BlockSpec((bm, d), lambda i: (i, 0)), _layer_row(layer, d)],
        out_specs=pl.BlockSpec((bm, d), lambda i: (i, 0)),
        compiler_params=_params(("parallel",), _vmem_estimate(
            [((bm, d), F32), ((bm, d), BF16)], temporaries=[((bm, d), F32)])),
        name="norm0",
    )(x, gains)


def _dot_nt(a, b):
    return lax.dot_general(a, b, (((1,), (1,)), ((), ())), preferred_element_type=F32)


def _in_proj_kernel(x_ref, wa_ref, wb_ref, o_ref, *, na):
    j = pl.program_id(0)

    @pl.when(j < na)
    def _():
        o_ref[...] = _dot_nt(x_ref[...], wa_ref[...])

    @pl.when(j >= na)
    def _():
        o_ref[...] = _dot_nt(x_ref[...], wb_ref[...])


def _in_proj(xn, wa_t, wb_t):
    m, d = xn.shape
    bm, bn = INPROJ_BM, INPROJ_BN
    na, nb = wa_t.shape[0] // bn, wb_t.shape[0] // bn
    assert na * bn == wa_t.shape[0] and nb * bn == wb_t.shape[0]
    return pl.pallas_call(
        functools.partial(_in_proj_kernel, na=na),
        out_shape=jax.ShapeDtypeStruct((m, (na + nb) * bn), F32),
        grid=(na + nb, m // bm),
        in_specs=[pl.BlockSpec((bm, d), lambda j, i: (i, 0)),
                  pl.BlockSpec((bn, d), lambda j, i: (jnp.minimum(j, na - 1), 0)),
                  pl.BlockSpec((bn, d), lambda j, i: (jnp.maximum(j - na, 0), 0))],
        out_specs=pl.BlockSpec((bm, bn), lambda j, i: (i, j)),
        compiler_params=_params(("parallel", "parallel"), _vmem_estimate(
            [((bm, d), BF16), ((bn, d), BF16), ((bn, d), BF16), ((bm, bn), F32)],
            temporaries=[((bm, bn), F32)])),
        name="in_proj",
    )(xn, wa_t, wb_t)


def _split3(x):
    hi = x.astype(BF16)
    r1 = x - hi.astype(F32)
    mid = r1.astype(BF16)
    lo = (r1 - mid.astype(F32)).astype(BF16)
    return hi, mid, lo


_BIAS_LANES = LANES // N_ATTN_HEADS


def _bias_selector():
    r = lax.broadcasted_iota(jnp.int32, (3 * LANES, 2 * LANES), 0)
    c = lax.broadcasted_iota(jnp.int32, (3 * LANES, 2 * LANES), 1)
    part, head = r // LANES, r % LANES
    target = jnp.where(c < LANES, _BIAS_LANES * head + part, LANES + _BIAS_LANES * head + 3 + part)
    return jnp.where((head < N_ATTN_HEADS) & (c == target), 1.0, 0.0).astype(BF16)


def _fox_prep_kernel(xn_ref, wf_ref, fb_ref, gq_ref, gk_ref):
    s = xn_ref.shape[0]
    t = CUMSUM_BLOCK
    inv_scale = HEAD_DIM ** 0.5
    f = _dot_nt(xn_ref[...], wf_ref[...].astype(BF16))
    lf = jax.nn.log_sigmoid(f + fb_ref[...])
    row = lax.broadcasted_iota(jnp.int32, (t, t), 0)
    col = lax.broadcasted_iota(jnp.int32, (t, t), 1)
    tri = jnp.where(row >= col, 1.0, 0.0).astype(BF16)
    sel = _bias_selector()
    slot = lax.broadcasted_iota(jnp.int32, (1, LANES), 1) % _BIAS_LANES
    ones_q = jnp.where((slot >= 3) & (slot < 6), 1.0, 0.0)
    ones_k = jnp.where(slot < 3, 1.0, 0.0)
    carry = jnp.zeros((1, LANES), F32)
    for b in range(s // t):
        hi, mid, lo = _split3(lf[b * t:(b + 1) * t, :])
        cb = (jnp.dot(tri, hi, preferred_element_type=F32)
              + jnp.dot(tri, mid, preferred_element_type=F32)
              + jnp.dot(tri, lo, preferred_element_type=F32)) + carry
        carry = cb[t - 1:t, :]
        parts = jnp.concatenate(_split3(cb * inv_scale), axis=1)
        placed = jnp.dot(parts, sel, preferred_element_type=F32)
        gq_ref[b * t:(b + 1) * t, :] = (placed[:, :LANES] + ones_q).astype(BF16)
        gk_ref[b * t:(b + 1) * t, :] = (ones_k - placed[:, LANES:]).astype(BF16)


def _fox_prep(xn3, wf, fb, layer):
    bsz, s, d = xn3.shape
    out = jax.ShapeDtypeStruct((bsz, s, LANES), BF16)
    return pl.pallas_call(
        _fox_prep_kernel,
        out_shape=(out, out),
        grid=(bsz,),
        in_specs=[pl.BlockSpec((None, s, d), lambda b: (b, 0, 0)),
                  pl.BlockSpec((None, LANES, d), lambda b: (layer, 0, 0)),
                  _layer_row(layer, LANES)],
        out_specs=(pl.BlockSpec((None, s, LANES), lambda b: (b, 0, 0)),
                   pl.BlockSpec((None, s, LANES), lambda b: (b, 0, 0))),
        compiler_params=_params(("parallel",), _vmem_estimate(
            [((s, d), BF16), ((LANES, d), F32), ((s, LANES), BF16), ((s, LANES), BF16)],
            temporaries=[((s, LANES), F32)] * 4)),
        name="fox_prep",
    )(xn3, wf, fb)


def _cast_specs(casts, steps, step_of):
    in_specs, out_specs, out_shapes, blocks = [], [], [], []
    for w, layer, row0, rows in casts:
        cols = w.shape[2]
        slab = rows // steps
        assert slab * steps == rows and slab % 16 == 0 and row0 % 8 == 0, (w.shape, row0, rows)

        def in_map(*idx, layer=layer, row0=row0, slab=slab):
            return layer, pl.multiple_of(row0 + slab * step_of(*idx), 8), 0

        in_specs.append(pl.BlockSpec((pl.Element(1), pl.Element(slab), pl.Element(cols)), in_map))
        out_specs.append(pl.BlockSpec((slab, cols), lambda *idx: (step_of(*idx), 0)))
        out_shapes.append(jax.ShapeDtypeStruct((rows, cols), BF16))
        blocks += [((slab, cols), F32), ((slab, cols), BF16)]
    return in_specs, out_specs, out_shapes, blocks


def _cast_kernel(*refs):
    n = len(refs) // 2
    for w32_ref, w16_ref in zip(refs[:n], refs[n:]):
        w16_ref[...] = w32_ref[0].astype(BF16)


CAST_STEPS = 32


def _cast_call(casts):
    in_specs, out_specs, out_shapes, blocks = _cast_specs(casts, CAST_STEPS, lambda t: t)
    return pl.pallas_call(
        _cast_kernel,
        out_shape=tuple(out_shapes),
        grid=(CAST_STEPS,),
        in_specs=in_specs,
        out_specs=tuple(out_specs),
        compiler_params=_params(("parallel",), _vmem_estimate(blocks)),
        name="cast_w_in0",
    )(*[c[0] for c in casts])


_LOG2E = 1.4426950408889634


def _attn_kernel(q_ref, k_ref, v_ref, gq_ref, gk_ref, qg_ref, kg_ref, *rest, tq, n_cast):
    w32_refs, o_ref, w16_refs = rest[:n_cast], rest[n_cast], rest[n_cast + 1:2 * n_cast + 1]
    ka_ref, vb_ref = rest[2 * n_cast + 1:]
    for w32_ref, w16_ref in zip(w32_refs, w16_refs):
        w16_ref[...] = w32_ref[0].astype(BF16)

    h = pl.program_id(1)
    s_len = q_ref.shape[0]
    scale = HEAD_DIM ** -0.5
    ka_ref[:, :HEAD_DIM] = _rms(k_ref[...], kg_ref[...]).astype(BF16)
    ka_ref[:, HEAD_DIM:] = gk_ref[...]
    vb_ref[...] = v_ref[...].astype(BF16)
    lane = lax.broadcasted_iota(jnp.int32, (tq, LANES), 1)
    own_lanes = (lane >= _BIAS_LANES * h) & (lane < _BIAS_LANES * (h + 1))
    causal = (lax.broadcasted_iota(jnp.int32, (tq, tq), 0) >= lax.broadcasted_iota(jnp.int32, (tq, tq), 1))
    for i in range(s_len // tq):
        r0, r1 = i * tq, (i + 1) * tq
        qn = _rms(q_ref[r0:r1, :], qg_ref[...]).astype(BF16)
        gq = jnp.where(own_lanes, gq_ref[r0:r1, :], jnp.zeros((), BF16))
        qa = jnp.concatenate([qn, gq], axis=1)
        sd = jnp.where(causal, _dot_nt(qa, ka_ref[r0:r1, :]), -jnp.inf)
        m = jnp.max(sd, axis=1, keepdims=True)
        if i > 0:
            so = _dot_nt(qa, ka_ref[0:r0, :])
            m = jnp.maximum(m, jnp.max(so, axis=1, keepdims=True))
        pd = jnp.exp2((sd - m) * (scale * _LOG2E))
        l = jnp.sum(pd, axis=1, keepdims=True)
        acc = jnp.dot(pd.astype(BF16), vb_ref[r0:r1, :], preferred_element_type=F32)
        if i > 0:
            po = jnp.exp2((so - m) * (scale * _LOG2E))
            l = l + jnp.sum(po, axis=1, keepdims=True)
            acc = acc + jnp.dot(po.astype(BF16), vb_ref[0:r0, :], preferred_element_type=F32)
        o_ref[r0:r1, :] = (acc / l).astype(o_ref.dtype)


def _attention(proj3, gq, gk, qg, kg, layer, casts):
    bsz, s, _ = proj3.shape
    tq = ATTN_TQ
    steps = bsz * N_ATTN_HEADS
    kernel = functools.partial(_attn_kernel, tq=tq, n_cast=len(casts))
    blk = (None, s, HEAD_DIM)
    gain = pl.BlockSpec((None, 1, HEAD_DIM), lambda b, h: (layer, 0, 0))
    bias = pl.BlockSpec((None, s, LANES), lambda b, h: (b, 0, 0))
    scratch = [((s, HEAD_DIM + LANES), BF16), ((s, HEAD_DIM), BF16)]
    step = lambda b, h: b * N_ATTN_HEADS + h
    cast_in, cast_out, cast_shapes, cast_blocks = _cast_specs(casts, steps, step)
    weights32 = [c[0] for c in casts]
    outs = pl.pallas_call(
        kernel,
        out_shape=(jax.ShapeDtypeStruct((bsz, s, ATTN_WIDTH), BF16), *cast_shapes),
        grid=(bsz, N_ATTN_HEADS),
        in_specs=[
            pl.BlockSpec(blk, lambda b, h: (b, 0, _QB + h)),
            pl.BlockSpec(blk, lambda b, h: (b, 0, _KB + h)),
            pl.BlockSpec(blk, lambda b, h: (b, 0, _VB + h)),
            bias, bias, gain, gain, *cast_in,
        ],
        out_specs=(pl.BlockSpec(blk, lambda b, h: (b, 0, h)), *cast_out),
        scratch_shapes=[pltpu.VMEM(shape, dtype) for shape, dtype in scratch],
        compiler_params=_params(("parallel", "parallel"), _vmem_estimate(
            [((s, HEAD_DIM), F32)] * 3 + [((s, HEAD_DIM), BF16)] * 3 + cast_blocks,
            resident=scratch,
            temporaries=[((s, HEAD_DIM), F32)] * 4 + [((tq, s), F32)] * 4)),
        name="fox_attn",
    )(proj3, proj3, proj3, gq, gk, qg, kg, *weights32)
    return outs[0], outs[1:]


def _mixers_kernel(gu_ref, gv_ref, xp_ref, vg_ref, ws_ref, bs_ref, wp_ref, ps_ref, yg_ref, yp_ref):
    g = pl.program_id(1)
    s = gu_ref.shape[0]

    vn = _rms(jax.nn.gelu(gv_ref[...]), vg_ref[...]).astype(BF16)
    row = lax.broadcasted_iota(jnp.int32, (CHUNK, CHUNK), 0)
    col = lax.broadcasted_iota(jnp.int32, (CHUNK, CHUNK), 1)
    w = jnp.where(row >= col, ws_ref[...], 0.0).astype(BF16)
    bias = bs_ref[...]
    for n in range(s // CHUNK):
        sl = slice(n * CHUNK, (n + 1) * CHUNK)
        mixed = jnp.dot(w, vn[sl, :], preferred_element_type=F32) + bias
        yg_ref[sl, :] = (jax.nn.gelu(gu_ref[sl, :]) * mixed).astype(yg_ref.dtype)

    x = xp_ref[...]
    t = lax.broadcasted_iota(jnp.int32, x.shape, 0)
    ws = x
    acc = x
    for level in range(len(POOL_WINDOWS)):
        shift = 1 << level
        acc = acc + jnp.where(t >= shift, pltpu.roll(acc, shift, 0), 0.0)
        ws = jnp.where(g == level, acc, ws)
    window = jnp.left_shift(2, g)
    cnt = jnp.minimum(t + 1, window).astype(F32)
    d = (ws / cnt - x).astype(BF16)
    y = jnp.dot(d, wp_ref[...], preferred_element_type=F32) * ps_ref[...]
    yp_ref[...] = y.astype(yp_ref.dtype)


def _mixers(proj3, v_gain, w_s, b_rep, w_pool, p_scale, layer):
    bsz, s, _ = proj3.shape
    col = lambda base: (lambda b, g: (b, 0, base + g))
    per_group = lambda b, g: (layer, g, 0, 0)
    blk = (None, s, HEAD_DIM)
    return pl.pallas_call(
        _mixers_kernel,
        out_shape=(jax.ShapeDtypeStruct((bsz, s, GMLP_WIDTH), BF16),
                   jax.ShapeDtypeStruct((bsz, s, POOL_WIDTH), BF16)),
        grid=(bsz, N_GMLP_HEADS),
        in_specs=[
            pl.BlockSpec(blk, col(_GUB)), pl.BlockSpec(blk, col(_GVB)), pl.BlockSpec(blk, col(_XPB)),
            pl.BlockSpec((None, None, 1, HEAD_DIM), per_group),
            pl.BlockSpec((None, None, CHUNK, CHUNK), per_group),
            pl.BlockSpec((None, None, CHUNK, HEAD_DIM), per_group),
            pl.BlockSpec((None, None, HEAD_DIM, HEAD_DIM), per_group),
            pl.BlockSpec((None, None, 1, HEAD_DIM), per_group),
        ],
        out_specs=(pl.BlockSpec(blk, lambda b, g: (b, 0, g)), pl.BlockSpec(blk, lambda b, g: (b, 0, g))),
        compiler_params=_params(("parallel", "parallel"), _vmem_estimate(
            [((s, HEAD_DIM), F32)] * 3 + [((s, HEAD_DIM), BF16)] * 2 + [((CHUNK, CHUNK), F32)] * 3,
            temporaries=[((s, HEAD_DIM), F32)] * 8)),
        name="mixers",
    )(proj3, proj3, proj3, v_gain, w_s, b_rep, w_pool, p_scale)


def _out_proj_kernel(ya_ref, yg_ref, yp_ref, h_ref, w_ref, g_ref, h1_ref, xn_ref):
    a0, a1, a2 = ATTN_WIDTH, ATTN_WIDTH + GMLP_WIDTH, D_MODEL
    acc = jnp.dot(ya_ref[...], w_ref[0:a0, :], preferred_element_type=F32)
    acc += jnp.dot(yg_ref[...], w_ref[a0:a1, :], preferred_element_type=F32)
    acc += jnp.dot(yp_ref[...], w_ref[a1:a2, :], preferred_element_type=F32)
    h1 = h_ref[...] + acc
    h1_ref[...] = h1
    xn_ref[...] = _rms(h1, g_ref[...]).astype(xn_ref.dtype)


def _out_proj(ya, yg, yp, h, w, gains, layer):
    m, d = h.shape
    bm = OUT_BM
    row = lambda i: (i, 0)
    return pl.pallas_call(
        _out_proj_kernel,
        out_shape=(jax.ShapeDtypeStruct((m, d), F32), jax.ShapeDtypeStruct((m, d), BF16)),
        grid=(m // bm,),
        in_specs=[pl.BlockSpec((bm, ATTN_WIDTH), row), pl.BlockSpec((bm, GMLP_WIDTH), row),
                  pl.BlockSpec((bm, POOL_WIDTH), row), pl.BlockSpec((bm, d), row),
                  pl.BlockSpec((d, d), lambda i: (0, 0)), _layer_row(layer, d)],
        out_specs=(pl.BlockSpec((bm, d), row), pl.BlockSpec((bm, d), row)),
        compiler_params=_params(("parallel",), _vmem_estimate(
            [((bm, d), BF16), ((bm, d), F32), ((d, d), BF16), ((bm, d), F32), ((bm, d), BF16)],
            temporaries=[((bm, d), F32)] * 2)),
        name="out_proj",
    )(ya, yg, yp, h, w, gains)


def _ffn_kernel(xn_ref, wg_ref, wu_ref, wd_ref, o_ref):
    @pl.when(pl.program_id(1) == 0)
    def _():
        o_ref[...] = jnp.zeros_like(o_ref)

    x = xn_ref[...]
    gate = jnp.dot(x, wg_ref[...], preferred_element_type=F32)
    up = jnp.dot(x, wu_ref[...], preferred_element_type=F32)
    a = (jax.nn.silu(gate) * up).astype(BF16)
    o_ref[...] += jnp.dot(a, wd_ref[...], preferred_element_type=F32)


def _ffn(xn, wg, wu, wd):
    m, d = xn.shape
    ff = wg.shape[1]
    bm, bf = FFN_BM, FFN_BF
    return pl.pallas_call(
        _ffn_kernel,
        out_shape=jax.ShapeDtypeStruct((m, d), F32),
        grid=(m // bm, ff // bf),
        in_specs=[pl.BlockSpec((bm, d), lambda i, j: (i, 0)),
                  pl.BlockSpec((d, bf), lambda i, j: (0, j)),
                  pl.BlockSpec((d, bf), lambda i, j: (0, j)),
                  pl.BlockSpec((bf, d), lambda i, j: (j, 0))],
        out_specs=pl.BlockSpec((bm, d), lambda i, j: (i, 0)),
        compiler_params=_params(("parallel", "arbitrary"), _vmem_estimate(
            [((bm, d), BF16), ((d, bf), BF16), ((d, bf), BF16), ((bf, d), BF16), ((bm, d), F32)],
            temporaries=[((bm, bf), F32)] * 3 + [((bm, d), F32)])),
        name="ffn",
    )(xn, wg, wu, wd)


def _ple_kernel(h1_ref, f_ref, p_ref, wg_ref, wp_ref, gp_ref, *rest):
    h2 = h1_ref[...] + f_ref[...]
    xn = _rms(h2, gp_ref[...]).astype(BF16)
    gate = jax.nn.sigmoid(jnp.dot(xn, wg_ref[...], preferred_element_type=F32))
    pe = jnp.dot(p_ref[...].astype(BF16), wp_ref[...], preferred_element_type=F32)
    h3 = h2 + pe * gate
    if len(rest) == 1:
        (h3_ref,) = rest
    else:
        gn_ref, h3_ref, xn_ref = rest
        xn_ref[...] = _rms(h3, gn_ref[...]).astype(BF16)
    h3_ref[...] = h3


def _ple(h1, f, p, wg, wp, gains_ple, gains_mix, layer, with_next):
    m, d = h1.shape
    bm = PLE_BM
    row = lambda i: (i, 0)
    in_specs = [pl.BlockSpec((bm, d), row), pl.BlockSpec((bm, d), row),
                pl.BlockSpec((None, bm, D_PLE), lambda i: (layer, i, 0)),
                pl.BlockSpec((d, d), lambda i: (0, 0)),
                pl.BlockSpec((None, D_PLE, d), lambda i: (layer, 0, 0)),
                _layer_row(layer, d)]
    args = [h1, f, p, wg, wp, gains_ple]
    out_shape = [jax.ShapeDtypeStruct((m, d), F32)]
    out_specs = [pl.BlockSpec((bm, d), row)]
    blocks = [((bm, d), F32)] * 3 + [((bm, D_PLE), F32), ((d, d), BF16), ((D_PLE, d), BF16)]
    if with_next:
        in_specs.append(_layer_row(layer + 1, d))
        args.append(gains_mix)
        out_shape.append(jax.ShapeDtypeStruct((m, d), BF16))
        out_specs.append(pl.BlockSpec((bm, d), row))
        blocks.append(((bm, d), BF16))
    return pl.pallas_call(
        _ple_kernel,
        out_shape=tuple(out_shape),
        grid=(m // bm,),
        in_specs=in_specs,
        out_specs=tuple(out_specs),
        compiler_params=_params(("parallel",), _vmem_estimate(
            blocks, temporaries=[((bm, d), F32)] * 3)),
        name="ple",
    )(*args)


def kernel(x, p, norm_mix, w_in, q_norm, k_norm, forget_bias, gmlp_v_norm, gmlp_w_s, gmlp_b_s,
           pool_w, pool_scale, w_out, norm_ffn, w_ffn_gate, w_ffn_up, w_ffn_down,
           norm_ple, w_ple_gate, w_ple_proj):
    bsz, s, d = x.shape
    depth = w_in.shape[0]
    m = bsz * s
    w_in_t = jnp.swapaxes(w_in, 1, 2)
    w_f_t = jnp.pad(w_in_t[:, _F_LO:_F_HI, :], ((0, 0), (0, LANES - N_ATTN_HEADS), (0, 0)))
    w_in_casts = lambda layer: [(w_in_t, layer, 0, _F_LO), (w_in_t, layer, _F_HI, w_in_t.shape[1] - _F_HI)]
    wa16, wb16 = _cast_call(w_in_casts(0))
    f_bias = jnp.pad(forget_bias, ((0, 0), (0, LANES - N_ATTN_HEADS)))[:, None, :]
    b_rep = jnp.broadcast_to(gmlp_b_s[..., None], (depth, N_GMLP_HEADS, CHUNK, HEAD_DIM))
    v_gain = gmlp_v_norm[:, :, None, :]
    p_scale = pool_scale.reshape(depth, N_POOL_GROUPS, 1, HEAD_DIM)
    pool_w16 = pool_w.astype(BF16)
    w_pp16 = w_ple_proj.astype(BF16)
    later_weights = (w_out, w_ffn_gate, w_ffn_up, w_ffn_down, w_ple_gate)
    g_mix, g_ffn, g_ple = norm_mix[:, None, :], norm_ffn[:, None, :], norm_ple[:, None, :]
    g_q, g_k = q_norm[:, None, :], k_norm[:, None, :]
    p2 = p.reshape(depth, m, D_PLE)

    h = x.reshape(m, d)
    xn = _norm(h, g_mix, 0)
    for i in range(depth):
        with_next = i + 1 < depth
        proj3 = _in_proj(xn, wa16, wb16).reshape(bsz, s, D_MAIN)
        gq, gk = _fox_prep(xn.reshape(bsz, s, d), w_f_t, f_bias, i)
        casts = [(w, i, 0, w.shape[1]) for w in later_weights]
        if with_next:
            casts += w_in_casts(i + 1)
        ya, w16 = _attention(proj3, gq, gk, g_q, g_k, i, casts)
        w_out16, w_g16, w_u16, w_d16, w_pg16 = w16[:5]
        yg, yp = _mixers(proj3, v_gain, gmlp_w_s, b_rep, pool_w16, p_scale, i)
        h1, xn_ffn = _out_proj(ya.reshape(m, ATTN_WIDTH), yg.reshape(m, GMLP_WIDTH),
                               yp.reshape(m, POOL_WIDTH), h, w_out16, g_ffn, i)
        f = _ffn(xn_ffn, w_g16, w_u16, w_d16)
        outs = _ple(h1, f, p2, w_pg16, w_pp16, g_ple, g_mix, i, with_next)
        h = outs[0]
        if with_next:
            xn = outs[1]
            wa16, wb16 = w16[5:]
    return h.reshape(bsz, s, d)
```

```python
import functools

import jax
import jax.numpy as jnp
from jax import lax
from jax.experimental import pallas as pl
from jax.experimental.pallas import tpu as pltpu

D_MODEL = 2048
HEAD_DIM = 128
N_ATTN_HEADS = 8
ATTN_WIDTH = N_ATTN_HEADS * HEAD_DIM
N_GMLP_HEADS = 4
GMLP_WIDTH = N_GMLP_HEADS * HEAD_DIM
N_POOL_GROUPS = 4
POOL_WIDTH = N_POOL_GROUPS * HEAD_DIM
POOL_WINDOWS = (2, 4, 8, 16)
CHUNK = 128
D_PLE = 256
EPS = 1e-6
D_MAIN = 3 * ATTN_WIDTH + 2 * GMLP_WIDTH + POOL_WIDTH
_F_LO = 3 * ATTN_WIDTH
_F_HI = _F_LO + N_ATTN_HEADS

LANES = 128
VMEM_BYTES_V7X = 64 * 1024 * 1024

_QB, _KB, _VB = 0, N_ATTN_HEADS, 2 * N_ATTN_HEADS
_GUB = 3 * N_ATTN_HEADS
_GVB = _GUB + N_GMLP_HEADS
_XPB = _GVB + N_GMLP_HEADS

NORM_BM = 512
INPROJ_BM, INPROJ_BN = 1024, 1536
ATTN_TQ = 512
CUMSUM_BLOCK = 256
OUT_BM = 512
FFN_BM, FFN_BF = 1024, 512
PLE_BM = 512

F32 = jnp.float32
BF16 = jnp.bfloat16


def _params(semantics, vmem_bytes):
    assert vmem_bytes <= VMEM_BYTES_V7X
    return pltpu.CompilerParams(dimension_semantics=semantics, vmem_limit_bytes=int(vmem_bytes))


def _nbytes(shape, dtype):
    n = 1
    for s in shape:
        n *= s
    return n * jnp.dtype(dtype).itemsize


def _vmem_estimate(pipelined, resident=(), temporaries=()):
    total = 2 * sum(_nbytes(s, d) for s, d in pipelined)
    total += sum(_nbytes(s, d) for s, d in resident)
    total += sum(_nbytes(s, d) for s, d in temporaries)
    return total + 2 * 1024 * 1024


def _rms(x, gain):
    return x * lax.rsqrt(jnp.mean(x * x, axis=-1, keepdims=True) + EPS) * gain


def _layer_row(layer, width):
    return pl.BlockSpec((None, 1, width), lambda i: (layer, 0, 0))


def _norm_kernel(x_ref, g_ref, o_ref):
    o_ref[...] = _rms(x_ref[...], g_ref[...]).astype(o_ref.dtype)


def _norm(x, gains, layer):
    m, d = x.shape
    bm = NORM_BM
    return pl.pallas_call(
        _norm_kernel,
        out_shape=jax.ShapeDtypeStruct((m, d), BF16),
        grid=(m // bm,),
        in_specs=[pl.BlockSpec((bm, d), lambda i: (i, 0)), _layer_row(layer, d)],
        out_specs=pl.BlockSpec((bm, d), lambda i: (i, 0)),
        compiler_params=_params(("parallel",), _vmem_estimate(
            [((bm, d), F32), ((bm, d), BF16)], temporaries=[((bm, d), F32)])),
        name="norm0",
    )(x, gains)


def _dot_nt(a, b):
    return lax.dot_general(a, b, (((1,), (1,)), ((), ())), preferred_element_type=F32)


def _in_proj_kernel(x_ref, wa_ref, wb_ref, o_ref, *, na):
    j = pl.program_id(0)

    @pl.when(j < na)
    def _():
        o_ref[...] = _dot_nt(x_ref[...], wa_ref[...])

    @pl.when(j >= na)
    def _():
        o_ref[...] = _dot_nt(x_ref[...], wb_ref[...])


def _in_proj(xn, wa_t, wb_t):
    m, d = xn.shape
    bm, bn = INPROJ_BM, INPROJ_BN
    na, nb = wa_t.shape[0] // bn, wb_t.shape[0] // bn
    assert na * bn == wa_t.shape[0] and nb * bn == wb_t.shape[0]
    return pl.pallas_call(
        functools.partial(_in_proj_kernel, na=na),
        out_shape=jax.ShapeDtypeStruct((m, (na + nb) * bn), F32),
        grid=(na + nb, m // bm),
        in_specs=[pl.BlockSpec((bm, d), lambda j, i: (i, 0)),
                  pl.BlockSpec((bn, d), lambda j, i: (jnp.minimum(j, na - 1), 0)),
                  pl.BlockSpec((bn, d), lambda j, i: (jnp.maximum(j - na, 0), 0))],
        out_specs=pl.BlockSpec((bm, bn), lambda j, i: (i, j)),
        compiler_params=_params(("parallel", "parallel"), _vmem_estimate(
            [((bm, d), BF16), ((bn, d), BF16), ((bn, d), BF16), ((bm, bn), F32)],
            temporaries=[((bm, bn), F32)])),
        name="in_proj",
    )(xn, wa_t, wb_t)


def _split3(x):
    hi = x.astype(BF16)
    r1 = x - hi.astype(F32)
    mid = r1.astype(BF16)
    lo = (r1 - mid.astype(F32)).astype(BF16)
    return hi, mid, lo


_BIAS_LANES = LANES // N_ATTN_HEADS


def _bias_selector():
    r = lax.broadcasted_iota(jnp.int32, (3 * LANES, 2 * LANES), 0)
    c = lax.broadcasted_iota(jnp.int32, (3 * LANES, 2 * LANES), 1)
    part, head = r // LANES, r % LANES
    target = jnp.where(c < LANES, _BIAS_LANES * head + part, LANES + _BIAS_LANES * head + 3 + part)
    return jnp.where((head < N_ATTN_HEADS) & (c == target), 1.0, 0.0).astype(BF16)


def _fox_prep_kernel(xn_ref, wf_ref, fb_ref, gq_ref, gk_ref):
    s = xn_ref.shape[0]
    t = CUMSUM_BLOCK
    inv_scale = HEAD_DIM ** 0.5
    f = _dot_nt(xn_ref[...], wf_ref[...].astype(BF16))
    lf = jax.nn.log_sigmoid(f + fb_ref[...])
    row = lax.broadcasted_iota(jnp.int32, (t, t), 0)
    col = lax.broadcasted_iota(jnp.int32, (t, t), 1)
    tri = jnp.where(row >= col, 1.0, 0.0).astype(BF16)
    sel = _bias_selector()
    slot = lax.broadcasted_iota(jnp.int32, (1, LANES), 1) % _BIAS_LANES
    ones_q = jnp.where((slot >= 3) & (slot < 6), 1.0, 0.0)
    ones_k = jnp.where(slot < 3, 1.0, 0.0)
    carry = jnp.zeros((1, LANES), F32)
    for b in range(s // t):
        hi, mid, lo = _split3(lf[b * t:(b + 1) * t, :])
        cb = (jnp.dot(tri, hi, preferred_element_type=F32)
              + jnp.dot(tri, mid, preferred_element_type=F32)
              + jnp.dot(tri, lo, preferred_element_type=F32)) + carry
        carry = cb[t - 1:t, :]
        parts = jnp.concatenate(_split3(cb * inv_scale), axis=1)
        placed = jnp.dot(parts, sel, preferred_element_type=F32)
        gq_ref[b * t:(b + 1) * t, :] = (placed[:, :LANES] + ones_q).astype(BF16)
        gk_ref[b * t:(b + 1) * t, :] = (ones_k - placed[:, LANES:]).astype(BF16)


def _fox_prep(xn3, wf, fb, layer):
    bsz, s, d = xn3.shape
    out = jax.ShapeDtypeStruct((bsz, s, LANES), BF16)
    return pl.pallas_call(
        _fox_prep_kernel,
        out_shape=(out, out),
        grid=(bsz,),
        in_specs=[pl.BlockSpec((None, s, d), lambda b: (b, 0, 0)),
                  pl.BlockSpec((None, LANES, d), lambda b: (layer, 0, 0)),
                  _layer_row(layer, LANES)],
        out_specs=(pl.BlockSpec((None, s, LANES), lambda b: (b, 0, 0)),
                   pl.BlockSpec((None, s, LANES), lambda b: (b, 0, 0))),
        compiler_params=_params(("parallel",), _vmem_estimate(
            [((s, d), BF16), ((LANES, d), F32), ((s, LANES), BF16), ((s, LANES), BF16)],
            temporaries=[((s, LANES), F32)] * 4)),
        name="fox_prep",
    )(xn3, wf, fb)


def _cast_specs(casts, steps, step_of):
    in_specs, out_specs, out_shapes, blocks = [], [], [], []
    for w, layer, row0, rows in casts:
        cols = w.shape[2]
        slab = rows // steps
        assert slab * steps == rows and slab % 16 == 0 and row0 % 8 == 0, (w.shape, row0, rows)

        def in_map(*idx, layer=layer, row0=row0, slab=slab):
            return layer, pl.multiple_of(row0 + slab * step_of(*idx), 8), 0

        in_specs.append(pl.BlockSpec((pl.Element(1), pl.Element(slab), pl.Element(cols)), in_map))
        out_specs.append(pl.BlockSpec((slab, cols), lambda *idx: (step_of(*idx), 0)))
        out_shapes.append(jax.ShapeDtypeStruct((rows, cols), BF16))
        blocks += [((slab, cols), F32), ((slab, cols), BF16)]
    return in_specs, out_specs, out_shapes, blocks


def _cast_kernel(*refs):
    n = len(refs) // 2
    for w32_ref, w16_ref in zip(refs[:n], refs[n:]):
        w16_ref[...] = w32_ref[0].astype(BF16)


CAST_STEPS = 32


def _cast_call(casts):
    in_specs, out_specs, out_shapes, blocks = _cast_specs(casts, CAST_STEPS, lambda t: t)
    return pl.pallas_call(
        _cast_kernel,
        out_shape=tuple(out_shapes),
        grid=(CAST_STEPS,),
        in_specs=in_specs,
        out_specs=tuple(out_specs),
        compiler_params=_params(("parallel",), _vmem_estimate(blocks)),
        name="cast_w_in0",
    )(*[c[0] for c in casts])


_LOG2E = 1.4426950408889634


def _attn_kernel(q_ref, k_ref, v_ref, gq_ref, gk_ref, qg_ref, kg_ref, *rest, tq, n_cast):
    w32_refs, o_ref, w16_refs = rest[:n_cast], rest[n_cast], rest[n_cast + 1:2 * n_cast + 1]
    ka_ref, vt_ref = rest[2 * n_cast + 1:]
    for w32_ref, w16_ref in zip(w32_refs, w16_refs):
        w16_ref[...] = w32_ref[0].astype(BF16)

    h = pl.program_id(1)
    s_len = q_ref.shape[0]
    scale = HEAD_DIM ** -0.5
    lane = lax.broadcasted_iota(jnp.int32, (tq, LANES), 1)
    own_lanes = (lane >= _BIAS_LANES * h) & (lane < _BIAS_LANES * (h + 1))
    causal = (lax.broadcasted_iota(jnp.int32, (tq, tq), 0) <= lax.broadcasted_iota(jnp.int32, (tq, tq), 1))
    for i in range(s_len // tq):
        r0, r1 = i * tq, (i + 1) * tq
        ka_ref[r0:r1, :HEAD_DIM] = _rms(k_ref[r0:r1, :], kg_ref[...]).astype(BF16)
        ka_ref[r0:r1, HEAD_DIM:] = gk_ref[r0:r1, :]
        vt_ref[:, r0:r1] = v_ref[r0:r1, :].T.astype(BF16)
        qn = _rms(q_ref[r0:r1, :], qg_ref[...]).astype(BF16)
        gq = jnp.where(own_lanes, gq_ref[r0:r1, :], jnp.zeros((), BF16))
        qa = jnp.concatenate([qn, gq], axis=1)
        sd = jnp.where(causal, _dot_nt(ka_ref[r0:r1, :], qa), -jnp.inf)
        m = jnp.max(sd, axis=0, keepdims=True)
        if i > 0:
            so = _dot_nt(ka_ref[0:r0, :], qa)
            m = jnp.maximum(m, jnp.max(so, axis=0, keepdims=True))
        pd = jnp.exp2((sd - m) * (scale * _LOG2E))
        l = jnp.sum(pd, axis=0, keepdims=True)
        acc = jnp.dot(vt_ref[:, r0:r1], pd.astype(BF16), preferred_element_type=F32)
        if i > 0:
            po = jnp.exp2((so - m) * (scale * _LOG2E))
            l = l + jnp.sum(po, axis=0, keepdims=True)
            acc = acc + jnp.dot(vt_ref[:, 0:r0], po.astype(BF16), preferred_element_type=F32)
        o_ref[r0:r1, :] = (acc / l).T.astype(o_ref.dtype)


def _attention(proj3, gq, gk, qg, kg, layer, casts):
    bsz, s, _ = proj3.shape
    tq = ATTN_TQ
    steps = bsz * N_ATTN_HEADS
    kernel = functools.partial(_attn_kernel, tq=tq, n_cast=len(casts))
    blk = (None, s, HEAD_DIM)
    gain = pl.BlockSpec((None, 1, HEAD_DIM), lambda b, h: (layer, 0, 0))
    bias = pl.BlockSpec((None, s, LANES), lambda b, h: (b, 0, 0))
    scratch = [((s, HEAD_DIM + LANES), BF16), ((HEAD_DIM, s), BF16)]
    step = lambda b, h: b * N_ATTN_HEADS + h
    cast_in, cast_out, cast_shapes, cast_blocks = _cast_specs(casts, steps, step)
    weights32 = [c[0] for c in casts]
    outs = pl.pallas_call(
        kernel,
        out_shape=(jax.ShapeDtypeStruct((bsz, s, ATTN_WIDTH), BF16), *cast_shapes),
        grid=(bsz, N_ATTN_HEADS),
        in_specs=[
            pl.BlockSpec(blk, lambda b, h: (b, 0, _QB + h)),
            pl.BlockSpec(blk, lambda b, h: (b, 0, _KB + h)),
            pl.BlockSpec(blk, lambda b, h: (b, 0, _VB + h)),
            bias, bias, gain, gain, *cast_in,
        ],
        out_specs=(pl.BlockSpec(blk, lambda b, h: (b, 0, h)), *cast_out),
        scratch_shapes=[pltpu.VMEM(shape, dtype) for shape, dtype in scratch],
        compiler_params=_params(("parallel", "parallel"), _vmem_estimate(
            [((s, HEAD_DIM), F32)] * 3 + [((s, HEAD_DIM), BF16)] * 3 + cast_blocks,
            resident=scratch,
            temporaries=[((s, HEAD_DIM), F32)] * 4 + [((tq, s), F32)] * 4)),
        name="fox_attn",
    )(proj3, proj3, proj3, gq, gk, qg, kg, *weights32)
    return outs[0], outs[1:]


def _mixers_kernel(gu_ref, gv_ref, xp_ref, vg_ref, ws_ref, bs_ref, wp_ref, ps_ref, yg_ref, yp_ref):
    g = pl.program_id(1)
    s = gu_ref.shape[0]

    vn = _rms(jax.nn.gelu(gv_ref[...]), vg_ref[...]).astype(BF16)
    row = lax.broadcasted_iota(jnp.int32, (CHUNK, CHUNK), 0)
    col = lax.broadcasted_iota(jnp.int32, (CHUNK, CHUNK), 1)
    w = jnp.where(row >= col, ws_ref[...], 0.0).astype(BF16)
    bias = bs_ref[...]
    for n in range(s // CHUNK):
        sl = slice(n * CHUNK, (n + 1) * CHUNK)
        mixed = jnp.dot(w, vn[sl, :], preferred_element_type=F32) + bias
        yg_ref[sl, :] = (jax.nn.gelu(gu_ref[sl, :]) * mixed).astype(yg_ref.dtype)

    x = xp_ref[...]
    t = lax.broadcasted_iota(jnp.int32, x.shape, 0)
    ws = x
    acc = x
    for level in range(len(POOL_WINDOWS)):
        shift = 1 << level
        acc = acc + jnp.where(t >= shift, pltpu.roll(acc, shift, 0), 0.0)
        ws = jnp.where(g == level, acc, ws)
    window = jnp.left_shift(2, g)
    cnt = jnp.minimum(t + 1, window).astype(F32)
    d = (ws / cnt - x).astype(BF16)
    y = jnp.dot(d, wp_ref[...], preferred_element_type=F32) * ps_ref[...]
    yp_ref[...] = y.astype(yp_ref.dtype)


def _mixers(proj3, v_gain, w_s, b_rep, w_pool, p_scale, layer):
    bsz, s, _ = proj3.shape
    col = lambda base: (lambda b, g: (b, 0, base + g))
    per_group = lambda b, g: (layer, g, 0, 0)
    blk = (None, s, HEAD_DIM)
    return pl.pallas_call(
        _mixers_kernel,
        out_shape=(jax.ShapeDtypeStruct((bsz, s, GMLP_WIDTH), BF16),
                   jax.ShapeDtypeStruct((bsz, s, POOL_WIDTH), BF16)),
        grid=(bsz, N_GMLP_HEADS),
        in_specs=[
            pl.BlockSpec(blk, col(_GUB)), pl.BlockSpec(blk, col(_GVB)), pl.BlockSpec(blk, col(_XPB)),
            pl.BlockSpec((None, None, 1, HEAD_DIM), per_group),
            pl.BlockSpec((None, None, CHUNK, CHUNK), per_group),
            pl.BlockSpec((None, None, CHUNK, HEAD_DIM), per_group),
            pl.BlockSpec((None, None, HEAD_DIM, HEAD_DIM), per_group),
            pl.BlockSpec((None, None, 1, HEAD_DIM), per_group),
        ],
        out_specs=(pl.BlockSpec(blk, lambda b, g: (b, 0, g)), pl.BlockSpec(blk, lambda b, g: (b, 0, g))),
        compiler_params=_params(("parallel", "parallel"), _vmem_estimate(
            [((s, HEAD_DIM), F32)] * 3 + [((s, HEAD_DIM), BF16)] * 2 + [((CHUNK, CHUNK), F32)] * 3,
            temporaries=[((s, HEAD_DIM), F32)] * 8)),
        name="mixers",
    )(proj3, proj3, proj3, v_gain, w_s, b_rep, w_pool, p_scale)


def _out_proj_kernel(ya_ref, yg_ref, yp_ref, h_ref, w_ref, g_ref, h1_ref, xn_ref):
    a0, a1, a2 = ATTN_WIDTH, ATTN_WIDTH + GMLP_WIDTH, D_MODEL
    acc = jnp.dot(ya_ref[...], w_ref[0:a0, :], preferred_element_type=F32)
    acc += jnp.dot(yg_ref[...], w_ref[a0:a1, :], preferred_element_type=F32)
    acc += jnp.dot(yp_ref[...], w_ref[a1:a2, :], preferred_element_type=F32)
    h1 = h_ref[...] + acc
    h1_ref[...] = h1
    xn_ref[...] = _rms(h1, g_ref[...]).astype(xn_ref.dtype)


def _out_proj(ya, yg, yp, h, w, gains, layer):
    m, d = h.shape
    bm = OUT_BM
    row = lambda i: (i, 0)
    return pl.pallas_call(
        _out_proj_kernel,
        out_shape=(jax.ShapeDtypeStruct((m, d), F32), jax.ShapeDtypeStruct((m, d), BF16)),
        grid=(m // bm,),
        in_specs=[pl.BlockSpec((bm, ATTN_WIDTH), row), pl.BlockSpec((bm, GMLP_WIDTH), row),
                  pl.BlockSpec((bm, POOL_WIDTH), row), pl.BlockSpec((bm, d), row),
                  pl.BlockSpec((d, d), lambda i: (0, 0)), _layer_row(layer, d)],
        out_specs=(pl.BlockSpec((bm, d), row), pl.BlockSpec((bm, d), row)),
        compiler_params=_params(("parallel",), _vmem_estimate(
            [((bm, d), BF16), ((bm, d), F32), ((d, d), BF16), ((bm, d), F32), ((bm, d), BF16)],
            temporaries=[((bm, d), F32)] * 2)),
        name="out_proj",
    )(ya, yg, yp, h, w, gains)


def _ffn_kernel(xn_ref, wg_ref, wu_ref, wd_ref, o_ref):
    @pl.when(pl.program_id(1) == 0)
    def _():
        o_ref[...] = jnp.zeros_like(o_ref)

    x = xn_ref[...]
    gate = jnp.dot(x, wg_ref[...], preferred_element_type=F32)
    up = jnp.dot(x, wu_ref[...], preferred_element_type=F32)
    a = (jax.nn.silu(gate) * up).astype(BF16)
    o_ref[...] += jnp.dot(a, wd_ref[...], preferred_element_type=F32)


def _ffn(xn, wg, wu, wd):
    m, d = xn.shape
    ff = wg.shape[1]
    bm, bf = FFN_BM, FFN_BF
    return pl.pallas_call(
        _ffn_kernel,
        out_shape=jax.ShapeDtypeStruct((m, d), F32),
        grid=(m // bm, ff // bf),
        in_specs=[pl.BlockSpec((bm, d), lambda i, j: (i, 0)),
                  pl.BlockSpec((d, bf), lambda i, j: (0, j)),
                  pl.BlockSpec((d, bf), lambda i, j: (0, j)),
                  pl.BlockSpec((bf, d), lambda i, j: (j, 0))],
        out_specs=pl.BlockSpec((bm, d), lambda i, j: (i, 0)),
        compiler_params=_params(("parallel", "arbitrary"), _vmem_estimate(
            [((bm, d), BF16), ((d, bf), BF16), ((d, bf), BF16), ((bf, d), BF16), ((bm, d), F32)],
            temporaries=[((bm, bf), F32)] * 3 + [((bm, d), F32)])),
        name="ffn",
    )(xn, wg, wu, wd)


def _ple_kernel(h1_ref, f_ref, p_ref, wg_ref, wp_ref, gp_ref, *rest):
    h2 = h1_ref[...] + f_ref[...]
    xn = _rms(h2, gp_ref[...]).astype(BF16)
    gate = jax.nn.sigmoid(jnp.dot(xn, wg_ref[...], preferred_element_type=F32))
    pe = jnp.dot(p_ref[...].astype(BF16), wp_ref[...], preferred_element_type=F32)
    h3 = h2 + pe * gate
    if len(rest) == 1:
        (h3_ref,) = rest
    else:
        gn_ref, h3_ref, xn_ref = rest
        xn_ref[...] = _rms(h3, gn_ref[...]).astype(BF16)
    h3_ref[...] = h3


def _ple(h1, f, p, wg, wp, gains_ple, gains_mix, layer, with_next):
    m, d = h1.shape
    bm = PLE_BM
    row = lambda i: (i, 0)
    in_specs = [pl.BlockSpec((bm, d), row), pl.BlockSpec((bm, d), row),
                pl.BlockSpec((None, bm, D_PLE), lambda i: (layer, i, 0)),
                pl.BlockSpec((d, d), lambda i: (0, 0)),
                pl.BlockSpec((None, D_PLE, d), lambda i: (layer, 0, 0)),
                _layer_row(layer, d)]
    args = [h1, f, p, wg, wp, gains_ple]
    out_shape = [jax.ShapeDtypeStruct((m, d), F32)]
    out_specs = [pl.BlockSpec((bm, d), row)]
    blocks = [((bm, d), F32)] * 3 + [((bm, D_PLE), F32), ((d, d), BF16), ((D_PLE, d), BF16)]
    if with_next:
        in_specs.append(_layer_row(layer + 1, d))
        args.append(gains_mix)
        out_shape.append(jax.ShapeDtypeStruct((m, d), BF16))
        out_specs.append(pl.BlockSpec((bm, d), row))
        blocks.append(((bm, d), BF16))
    return pl.pallas_call(
        _ple_kernel,
        out_shape=tuple(out_shape),
        grid=(m // bm,),
        in_specs=in_specs,
        out_specs=tuple(out_specs),
        compiler_params=_params(("parallel",), _vmem_estimate(
            blocks, temporaries=[((bm, d), F32)] * 3)),
        name="ple",
    )(*args)


def kernel(x, p, norm_mix, w_in, q_norm, k_norm, forget_bias, gmlp_v_norm, gmlp_w_s, gmlp_b_s,
           pool_w, pool_scale, w_out, norm_ffn, w_ffn_gate, w_ffn_up, w_ffn_down,
           norm_ple, w_ple_gate, w_ple_proj):
    bsz, s, d = x.shape
    depth = w_in.shape[0]
    m = bsz * s
    w_in_t = jnp.swapaxes(w_in, 1, 2)
    w_f_t = jnp.pad(w_in_t[:, _F_LO:_F_HI, :], ((0, 0), (0, LANES - N_ATTN_HEADS), (0, 0)))
    w_in_casts = lambda layer: [(w_in_t, layer, 0, _F_LO), (w_in_t, layer, _F_HI, w_in_t.shape[1] - _F_HI)]
    wa16, wb16 = _cast_call(w_in_casts(0))
    f_bias = jnp.pad(forget_bias, ((0, 0), (0, LANES - N_ATTN_HEADS)))[:, None, :]
    b_rep = jnp.broadcast_to(gmlp_b_s[..., None], (depth, N_GMLP_HEADS, CHUNK, HEAD_DIM))
    v_gain = gmlp_v_norm[:, :, None, :]
    p_scale = pool_scale.reshape(depth, N_POOL_GROUPS, 1, HEAD_DIM)
    pool_w16 = pool_w.astype(BF16)
    w_pp16 = w_ple_proj.astype(BF16)
    later_weights = (w_out, w_ffn_gate, w_ffn_up, w_ffn_down, w_ple_gate)
    g_mix, g_ffn, g_ple = norm_mix[:, None, :], norm_ffn[:, None, :], norm_ple[:, None, :]
    g_q, g_k = q_norm[:, None, :], k_norm[:, None, :]
    p2 = p.reshape(depth, m, D_PLE)

    h = x.reshape(m, d)
    xn = _norm(h, g_mix, 0)
    for i in range(depth):
        with_next = i + 1 < depth
        proj3 = _in_proj(xn, wa16, wb16).reshape(bsz, s, D_MAIN)
        gq, gk = _fox_prep(xn.reshape(bsz, s, d), w_f_t, f_bias, i)
        casts = [(w, i, 0, w.shape[1]) for w in later_weights]
        if with_next:
            casts += w_in_casts(i + 1)
        ya, w16 = _attention(proj3, gq, gk, g_q, g_k, i, casts)
        w_out16, w_g16, w_u16, w_d16, w_pg16 = w16[:5]
        yg, yp = _mixers(proj3, v_gain, gmlp_w_s, b_rep, pool_w16, p_scale, i)
        h1, xn_ffn = _out_proj(ya.reshape(m, ATTN_WIDTH), yg.reshape(m, GMLP_WIDTH),
                               yp.reshape(m, POOL_WIDTH), h, w_out16, g_ffn, i)
        f = _ffn(xn_ffn, w_g16, w_u16, w_d16)
        outs = _ple(h1, f, p2, w_pg16, w_pp16, g_ple, g_mix, i, with_next)
        h = outs[0]
        if with_next:
            xn = outs[1]
            wa16, wb16 = w16[5:]
    return h.reshape(bsz, s, d)
```

```python
import functools

import jax
import jax.numpy as jnp
from jax import lax
from jax.experimental import pallas as pl
from jax.experimental.pallas import tpu as pltpu

D_MODEL = 2048
HEAD_DIM = 128
N_ATTN_HEADS = 8
ATTN_WIDTH = N_ATTN_HEADS * HEAD_DIM
N_GMLP_HEADS = 4
GMLP_WIDTH = N_GMLP_HEADS * HEAD_DIM
N_POOL_GROUPS = 4
POOL_WIDTH = N_POOL_GROUPS * HEAD_DIM
POOL_WINDOWS = (2, 4, 8, 16)
CHUNK = 128
D_PLE = 256
EPS = 1e-6
D_MAIN = 3 * ATTN_WIDTH + 2 * GMLP_WIDTH + POOL_WIDTH
_F_LO = 3 * ATTN_WIDTH
_F_HI = _F_LO + N_ATTN_HEADS

LANES = 128
VMEM_BYTES_V7X = 64 * 1024 * 1024

_QB, _KB, _VB = 0, N_ATTN_HEADS, 2 * N_ATTN_HEADS
_GUB = 3 * N_ATTN_HEADS
_GVB = _GUB + N_GMLP_HEADS
_XPB = _GVB + N_GMLP_HEADS

NORM_BM = 512
INPROJ_BM, INPROJ_BN = 1024, 1536
ATTN_TQ = 512
ATTN_HEADS_PER_STEP = 1
CUMSUM_BLOCK = 256
OUT_BM = 512
FFN_BM, FFN_BF = 1024, 512
PLE_BM = 512

F32 = jnp.float32
BF16 = jnp.bfloat16


def _params(semantics, vmem_bytes):
    assert vmem_bytes <= VMEM_BYTES_V7X
    return pltpu.CompilerParams(dimension_semantics=semantics, vmem_limit_bytes=int(vmem_bytes))


def _nbytes(shape, dtype):
    n = 1
    for s in shape:
        n *= s
    return n * jnp.dtype(dtype).itemsize


def _vmem_estimate(pipelined, resident=(), temporaries=()):
    total = 2 * sum(_nbytes(s, d) for s, d in pipelined)
    total += sum(_nbytes(s, d) for s, d in resident)
    total += sum(_nbytes(s, d) for s, d in temporaries)
    return total + 2 * 1024 * 1024


def _rms(x, gain):
    return x * lax.rsqrt(jnp.mean(x * x, axis=-1, keepdims=True) + EPS) * gain


def _layer_row(layer, width):
    return pl.BlockSpec((None, 1, width), lambda i: (layer, 0, 0))


def _norm_kernel(x_ref, g_ref, o_ref):
    o_ref[...] = _rms(x_ref[...], g_ref[...]).astype(o_ref.dtype)


def _norm(x, gains, layer):
    m, d = x.shape
    bm = NORM_BM
    return pl.pallas_call(
        _norm_kernel,
        out_shape=jax.ShapeDtypeStruct((m, d), BF16),
        grid=(m // bm,),
        in_specs=[pl.BlockSpec((bm, d), lambda i: (i, 0)), _layer_row(layer, d)],
        out_specs=pl.BlockSpec((bm, d), lambda i: (i, 0)),
        compiler_params=_params(("parallel",), _vmem_estimate(
            [((bm, d), F32), ((bm, d), BF16)], temporaries=[((bm, d), F32)])),
        name="norm0",
    )(x, gains)


def _dot_nt(a, b):
    return lax.dot_general(a, b, (((1,), (1,)), ((), ())), preferred_element_type=F32)


def _in_proj_kernel(x_ref, wa_ref, wb_ref, qg_ref, kg_ref, o_ref, *, na):
    j = pl.program_id(0)
    heads_per_block = o_ref.shape[1] // HEAD_DIM
    for jj in range(na):
        @pl.when(j == jj)
        def _(jj=jj):
            y = _dot_nt(x_ref[...], wa_ref[...])
            for c in range(heads_per_block):
                lanes = slice(c * HEAD_DIM, (c + 1) * HEAD_DIM)
                head = jj * heads_per_block + c
                if head < N_ATTN_HEADS:
                    o_ref[:, lanes] = _rms(y[:, lanes], qg_ref[...])
                elif head < 2 * N_ATTN_HEADS:
                    o_ref[:, lanes] = _rms(y[:, lanes], kg_ref[...])
                else:
                    o_ref[:, lanes] = y[:, lanes]

    @pl.when(j >= na)
    def _():
        o_ref[...] = _dot_nt(x_ref[...], wb_ref[...])


def _in_proj(xn, wa_t, wb_t, qg, kg, layer):
    m, d = xn.shape
    bm, bn = INPROJ_BM, INPROJ_BN
    na, nb = wa_t.shape[0] // bn, wb_t.shape[0] // bn
    assert na * bn == wa_t.shape[0] and nb * bn == wb_t.shape[0] and bn % HEAD_DIM == 0
    gain = pl.BlockSpec((None, 1, HEAD_DIM), lambda j, i: (layer, 0, 0))
    return pl.pallas_call(
        functools.partial(_in_proj_kernel, na=na),
        out_shape=jax.ShapeDtypeStruct((m, (na + nb) * bn), F32),
        grid=(na + nb, m // bm),
        in_specs=[pl.BlockSpec((bm, d), lambda j, i: (i, 0)),
                  pl.BlockSpec((bn, d), lambda j, i: (jnp.minimum(j, na - 1), 0)),
                  pl.BlockSpec((bn, d), lambda j, i: (jnp.maximum(j - na, 0), 0)),
                  gain, gain],
        out_specs=pl.BlockSpec((bm, bn), lambda j, i: (i, j)),
        compiler_params=_params(("parallel", "parallel"), _vmem_estimate(
            [((bm, d), BF16), ((bn, d), BF16), ((bn, d), BF16), ((bm, bn), F32)],
            temporaries=[((bm, bn), F32)])),
        name="in_proj",
    )(xn, wa_t, wb_t, qg, kg)


def _split3(x):
    hi = x.astype(BF16)
    r1 = x - hi.astype(F32)
    mid = r1.astype(BF16)
    lo = (r1 - mid.astype(F32)).astype(BF16)
    return hi, mid, lo


_BIAS_LANES = LANES // N_ATTN_HEADS


def _bias_selector():
    r = lax.broadcasted_iota(jnp.int32, (3 * LANES, 2 * LANES), 0)
    c = lax.broadcasted_iota(jnp.int32, (3 * LANES, 2 * LANES), 1)
    part, head = r // LANES, r % LANES
    target = jnp.where(c < LANES, _BIAS_LANES * head + part, LANES + _BIAS_LANES * head + 3 + part)
    return jnp.where((head < N_ATTN_HEADS) & (c == target), 1.0, 0.0).astype(BF16)


def _fox_prep_kernel(xn_ref, wf_ref, fb_ref, gq_ref, gk_ref):
    s = xn_ref.shape[0]
    t = CUMSUM_BLOCK
    inv_scale = HEAD_DIM ** 0.5
    f = _dot_nt(xn_ref[...], wf_ref[...].astype(BF16))
    lf = jax.nn.log_sigmoid(f + fb_ref[...])
    row = lax.broadcasted_iota(jnp.int32, (t, t), 0)
    col = lax.broadcasted_iota(jnp.int32, (t, t), 1)
    tri = jnp.where(row >= col, 1.0, 0.0).astype(BF16)
    sel = _bias_selector()
    slot = lax.broadcasted_iota(jnp.int32, (1, LANES), 1) % _BIAS_LANES
    ones_q = jnp.where((slot >= 3) & (slot < 6), 1.0, 0.0)
    ones_k = jnp.where(slot < 3, 1.0, 0.0)
    carry = jnp.zeros((1, LANES), F32)
    for b in range(s // t):
        hi, mid, lo = _split3(lf[b * t:(b + 1) * t, :])
        cb = (jnp.dot(tri, hi, preferred_element_type=F32)
              + jnp.dot(tri, mid, preferred_element_type=F32)
              + jnp.dot(tri, lo, preferred_element_type=F32)) + carry
        carry = cb[t - 1:t, :]
        parts = jnp.concatenate(_split3(cb * inv_scale), axis=1)
        placed = jnp.dot(parts, sel, preferred_element_type=F32)
        gq_ref[b * t:(b + 1) * t, :] = (placed[:, :LANES] + ones_q).astype(BF16)
        gk_ref[b * t:(b + 1) * t, :] = (ones_k - placed[:, LANES:]).astype(BF16)


def _fox_prep(xn3, wf, fb, layer):
    bsz, s, d = xn3.shape
    out = jax.ShapeDtypeStruct((bsz, s, LANES), BF16)
    return pl.pallas_call(
        _fox_prep_kernel,
        out_shape=(out, out),
        grid=(bsz,),
        in_specs=[pl.BlockSpec((None, s, d), lambda b: (b, 0, 0)),
                  pl.BlockSpec((None, LANES, d), lambda b: (layer, 0, 0)),
                  _layer_row(layer, LANES)],
        out_specs=(pl.BlockSpec((None, s, LANES), lambda b: (b, 0, 0)),
                   pl.BlockSpec((None, s, LANES), lambda b: (b, 0, 0))),
        compiler_params=_params(("parallel",), _vmem_estimate(
            [((s, d), BF16), ((LANES, d), F32), ((s, LANES), BF16), ((s, LANES), BF16)],
            temporaries=[((s, LANES), F32)] * 4)),
        name="fox_prep",
    )(xn3, wf, fb)


def _cast_specs(casts, steps, step_of):
    in_specs, out_specs, out_shapes, blocks = [], [], [], []
    for w, layer, row0, rows in casts:
        cols = w.shape[2]
        slab = rows // steps
        assert slab * steps == rows and slab % 16 == 0 and row0 % 8 == 0, (w.shape, row0, rows)

        def in_map(*idx, layer=layer, row0=row0, slab=slab):
            return layer, pl.multiple_of(row0 + slab * step_of(*idx), 8), 0

        in_specs.append(pl.BlockSpec((pl.Element(1), pl.Element(slab), pl.Element(cols)), in_map))
        out_specs.append(pl.BlockSpec((slab, cols), lambda *idx: (step_of(*idx), 0)))
        out_shapes.append(jax.ShapeDtypeStruct((rows, cols), BF16))
        blocks += [((slab, cols), F32), ((slab, cols), BF16)]
    return in_specs, out_specs, out_shapes, blocks


def _cast_kernel(*refs):
    n = len(refs) // 2
    for w32_ref, w16_ref in zip(refs[:n], refs[n:]):
        w16_ref[...] = w32_ref[0].astype(BF16)


CAST_STEPS = 32


def _cast_call(casts):
    in_specs, out_specs, out_shapes, blocks = _cast_specs(casts, CAST_STEPS, lambda t: t)
    return pl.pallas_call(
        _cast_kernel,
        out_shape=tuple(out_shapes),
        grid=(CAST_STEPS,),
        in_specs=in_specs,
        out_specs=tuple(out_specs),
        compiler_params=_params(("parallel",), _vmem_estimate(blocks)),
        name="cast_w_in0",
    )(*[c[0] for c in casts])


_LOG2E = 1.4426950408889634


def _attn_kernel(q_ref, k_ref, v_ref, gq_ref, gk_ref, *rest, tq, n_cast, hps):
    w32_refs, o_ref, w16_refs = rest[:n_cast], rest[n_cast], rest[n_cast + 1:2 * n_cast + 1]
    ka_ref, vt_ref = rest[2 * n_cast + 1:]
    for w32_ref, w16_ref in zip(w32_refs, w16_refs):
        w16_ref[...] = w32_ref[0].astype(BF16)

    s_len = q_ref.shape[0]
    scale = HEAD_DIM ** -0.5
    lane = lax.broadcasted_iota(jnp.int32, (tq, LANES), 1)
    causal = (lax.broadcasted_iota(jnp.int32, (tq, tq), 0) <= lax.broadcasted_iota(jnp.int32, (tq, tq), 1))
    for hh in range(hps):
        h = pl.program_id(1) * hps + hh
        cols = slice(hh * HEAD_DIM, (hh + 1) * HEAD_DIM)
        own_lanes = (lane >= _BIAS_LANES * h) & (lane < _BIAS_LANES * (h + 1))
        for i in range(s_len // tq):
            r0, r1 = i * tq, (i + 1) * tq
            ka_ref[hh, r0:r1, :HEAD_DIM] = k_ref[r0:r1, cols].astype(BF16)
            ka_ref[hh, r0:r1, HEAD_DIM:] = gk_ref[r0:r1, :]
            vt_ref[hh, :, r0:r1] = v_ref[r0:r1, cols].T.astype(BF16)
            qn = q_ref[r0:r1, cols].astype(BF16)
            gq = jnp.where(own_lanes, gq_ref[r0:r1, :], jnp.zeros((), BF16))
            qa = jnp.concatenate([qn, gq], axis=1)
            sd = jnp.where(causal, _dot_nt(ka_ref[hh, r0:r1, :], qa), -jnp.inf)
            m = jnp.max(sd, axis=0, keepdims=True)
            if i > 0:
                so = _dot_nt(ka_ref[hh, 0:r0, :], qa)
                m = jnp.maximum(m, jnp.max(so, axis=0, keepdims=True))
            pd = jnp.exp2((sd - m) * (scale * _LOG2E))
            l = jnp.sum(pd, axis=0, keepdims=True)
            acc = jnp.dot(vt_ref[hh, :, r0:r1], pd.astype(BF16), preferred_element_type=F32)
            if i > 0:
                po = jnp.exp2((so - m) * (scale * _LOG2E))
                l = l + jnp.sum(po, axis=0, keepdims=True)
                acc = acc + jnp.dot(vt_ref[hh, :, 0:r0], po.astype(BF16), preferred_element_type=F32)
            o_ref[r0:r1, cols] = (acc / l).T.astype(o_ref.dtype)


def _attention(proj3, gq, gk, casts):
    bsz, s, _ = proj3.shape
    tq, hps = ATTN_TQ, ATTN_HEADS_PER_STEP
    groups = N_ATTN_HEADS // hps
    steps = bsz * groups
    kernel = functools.partial(_attn_kernel, tq=tq, n_cast=len(casts), hps=hps)
    blk = (None, s, hps * HEAD_DIM)
    bias = pl.BlockSpec((None, s, LANES), lambda b, h: (b, 0, 0))
    scratch = [((hps, s, HEAD_DIM + LANES), BF16), ((hps, HEAD_DIM, s), BF16)]
    step = lambda b, h: b * groups + h
    cast_in, cast_out, cast_shapes, cast_blocks = _cast_specs(casts, steps, step)
    weights32 = [c[0] for c in casts]
    outs = pl.pallas_call(
        kernel,
        out_shape=(jax.ShapeDtypeStruct((bsz, s, ATTN_WIDTH), BF16), *cast_shapes),
        grid=(bsz, groups),
        in_specs=[
            pl.BlockSpec(blk, lambda b, h: (b, 0, _QB // hps + h)),
            pl.BlockSpec(blk, lambda b, h: (b, 0, _KB // hps + h)),
            pl.BlockSpec(blk, lambda b, h: (b, 0, _VB // hps + h)),
            bias, bias, *cast_in,
        ],
        out_specs=(pl.BlockSpec(blk, lambda b, h: (b, 0, h)), *cast_out),
        scratch_shapes=[pltpu.VMEM(shape, dtype) for shape, dtype in scratch],
        compiler_params=_params(("parallel", "parallel"), _vmem_estimate(
            [((s, hps * HEAD_DIM), F32)] * 3 + [((s, hps * HEAD_DIM), BF16)] + [((s, LANES), BF16)] * 2 + cast_blocks,
            resident=scratch,
            temporaries=[((tq, s), F32)] * 2 * hps)),
        name="fox_attn",
    )(proj3, proj3, proj3, gq, gk, *weights32)
    return outs[0], outs[1:]


POOL_HALO = 16
assert max(POOL_WINDOWS) - 1 <= POOL_HALO


def _mix_out_kernel(ya_ref, gu_ref, gv_ref, xp_ref, halo_ref, h_ref, w_ref, g_ref,
                    vg_ref, ws_ref, bs_ref, wp_ref, ps_ref, h1_ref, xn_ref, yg_ref, yp_ref, *, tiles_per_seq):
    i = pl.program_id(0)
    bm = h_ref.shape[0]
    a0, a1, a2 = ATTN_WIDTH, ATTN_WIDTH + GMLP_WIDTH, D_MODEL
    h1_ref[...] = h_ref[...] + jnp.dot(ya_ref[...], w_ref[0:a0, :], preferred_element_type=F32)

    row = lax.broadcasted_iota(jnp.int32, (CHUNK, CHUNK), 0)
    col = lax.broadcasted_iota(jnp.int32, (CHUNK, CHUNK), 1)
    for g in range(N_GMLP_HEADS):
        lanes = slice(g * HEAD_DIM, (g + 1) * HEAD_DIM)
        vn = _rms(jax.nn.gelu(gv_ref[:, lanes]), vg_ref[g]).astype(BF16)
        w = jnp.where(row >= col, ws_ref[g], 0.0).astype(BF16)
        for n in range(bm // CHUNK):
            rows = slice(n * CHUNK, (n + 1) * CHUNK)
            mixed = jnp.dot(w, vn[rows, :], preferred_element_type=F32) + bs_ref[g]
            yg_ref[rows, lanes] = (jax.nn.gelu(gu_ref[rows, lanes]) * mixed).astype(BF16)

    first_tile = (i % tiles_per_seq) == 0
    x = xp_ref[...]
    xe = jnp.concatenate([jnp.where(first_tile, 0.0, halo_ref[...]), x], axis=0)
    t = (i % tiles_per_seq) * bm + lax.broadcasted_iota(jnp.int32, (bm, HEAD_DIM), 0)
    level = xe
    for g, window in enumerate(POOL_WINDOWS):
        level = level[:, HEAD_DIM * min(g, 1):]
        level = level + pltpu.roll(level, window // 2, 0)
        cnt = jnp.minimum(t + 1, window).astype(F32)
        xg = x[:, g * HEAD_DIM:(g + 1) * HEAD_DIM]
        d = (level[POOL_HALO:, :HEAD_DIM] / cnt - xg).astype(BF16)
        y = jnp.dot(d, wp_ref[g], preferred_element_type=F32) * ps_ref[g]
        yp_ref[:, g * HEAD_DIM:(g + 1) * HEAD_DIM] = y.astype(BF16)

    acc = jnp.dot(yg_ref[...], w_ref[a0:a1, :], preferred_element_type=F32)
    acc += jnp.dot(yp_ref[...], w_ref[a1:a2, :], preferred_element_type=F32)
    h1 = h1_ref[...] + acc
    h1_ref[...] = h1
    xn_ref[...] = _rms(h1, g_ref[...]).astype(xn_ref.dtype)


def _mix_out(ya, proj, h, w, gains, v_gain, w_s, b_rep, w_pool, p_scale, layer, seq_len):
    m, d = h.shape
    bm = OUT_BM
    assert bm % CHUNK == 0 and seq_len % bm == 0 and bm % POOL_HALO == 0
    row = lambda i: (i, 0)
    wide = GMLP_WIDTH
    assert wide == POOL_WIDTH and (_GUB * LANES) % wide == 0 and (_GVB * LANES) % wide == 0 and (_XPB * LANES) % wide == 0
    col = lambda block: (lambda i: (i, block * LANES // wide))
    halo_rows = bm // POOL_HALO
    per_layer = lambda i: (layer, 0, 0, 0)
    scratch = [((bm, GMLP_WIDTH), BF16), ((bm, POOL_WIDTH), BF16)]
    return pl.pallas_call(
        functools.partial(_mix_out_kernel, tiles_per_seq=seq_len // bm),
        out_shape=(jax.ShapeDtypeStruct((m, d), F32), jax.ShapeDtypeStruct((m, d), BF16)),
        grid=(m // bm,),
        in_specs=[pl.BlockSpec((bm, ATTN_WIDTH), row),
                  pl.BlockSpec((bm, wide), col(_GUB)), pl.BlockSpec((bm, wide), col(_GVB)),
                  pl.BlockSpec((bm, wide), col(_XPB)),
                  pl.BlockSpec((POOL_HALO, wide),
                               lambda i: (jnp.maximum(i * halo_rows - 1, 0), _XPB * LANES // wide)),
                  pl.BlockSpec((bm, d), row),
                  pl.BlockSpec((d, d), lambda i: (0, 0)), _layer_row(layer, d),
                  pl.BlockSpec((None, N_GMLP_HEADS, 1, HEAD_DIM), per_layer),
                  pl.BlockSpec((None, N_GMLP_HEADS, CHUNK, CHUNK), per_layer),
                  pl.BlockSpec((None, N_GMLP_HEADS, CHUNK, HEAD_DIM), per_layer),
                  pl.BlockSpec((None, N_POOL_GROUPS, HEAD_DIM, HEAD_DIM), per_layer),
                  pl.BlockSpec((None, N_POOL_GROUPS, 1, HEAD_DIM), per_layer)],
        out_specs=(pl.BlockSpec((bm, d), row), pl.BlockSpec((bm, d), row)),
        scratch_shapes=[pltpu.VMEM(shape, dtype) for shape, dtype in scratch],
        compiler_params=_params(("parallel",), _vmem_estimate(
            [((bm, ATTN_WIDTH), BF16)] + [((bm, wide), F32)] * 3 + [((bm, d), F32), ((bm, d), F32), ((bm, d), BF16)]
            + [((N_GMLP_HEADS, CHUNK, CHUNK), F32)] * 3,
            resident=scratch + [((d, d), BF16)],
            temporaries=[((bm, d), F32)] + [((bm + POOL_HALO, wide), F32)] * 2)),
        name="mix_out",
    )(ya, proj, proj, proj, proj, h, w, gains, v_gain, w_s, b_rep, w_pool, p_scale)


def _ffn_kernel(xn_ref, wg_ref, wu_ref, wd_ref, o_ref):
    @pl.when(pl.program_id(1) == 0)
    def _():
        o_ref[...] = jnp.zeros_like(o_ref)

    x = xn_ref[...]
    gate = jnp.dot(x, wg_ref[...], preferred_element_type=F32)
    up = jnp.dot(x, wu_ref[...], preferred_element_type=F32)
    a = (jax.nn.silu(gate) * up).astype(BF16)
    o_ref[...] += jnp.dot(a, wd_ref[...], preferred_element_type=F32)


def _ffn(xn, wg, wu, wd):
    m, d = xn.shape
    ff = wg.shape[1]
    bm, bf = FFN_BM, FFN_BF
    return pl.pallas_call(
        _ffn_kernel,
        out_shape=jax.ShapeDtypeStruct((m, d), F32),
        grid=(m // bm, ff // bf),
        in_specs=[pl.BlockSpec((bm, d), lambda i, j: (i, 0)),
                  pl.BlockSpec((d, bf), lambda i, j: (0, j)),
                  pl.BlockSpec((d, bf), lambda i, j: (0, j)),
                  pl.BlockSpec((bf, d), lambda i, j: (j, 0))],
        out_specs=pl.BlockSpec((bm, d), lambda i, j: (i, 0)),
        compiler_params=_params(("parallel", "arbitrary"), _vmem_estimate(
            [((bm, d), BF16), ((d, bf), BF16), ((d, bf), BF16), ((bf, d), BF16), ((bm, d), F32)],
            temporaries=[((bm, bf), F32)] * 3)),
        name="ffn",
    )(xn, wg, wu, wd)


def _ple_kernel(h1_ref, f_ref, p_ref, wg_ref, wp_ref, gp_ref, *rest):
    h2 = h1_ref[...] + f_ref[...]
    xn = _rms(h2, gp_ref[...]).astype(BF16)
    gate = jax.nn.sigmoid(jnp.dot(xn, wg_ref[...], preferred_element_type=F32))
    pe = jnp.dot(p_ref[...].astype(BF16), wp_ref[...], preferred_element_type=F32)
    h3 = h2 + pe * gate
    if len(rest) == 1:
        (h3_ref,) = rest
    else:
        gn_ref, h3_ref, xn_ref = rest
        xn_ref[...] = _rms(h3, gn_ref[...]).astype(BF16)
    h3_ref[...] = h3


def _ple(h1, f, p, wg, wp, gains_ple, gains_mix, layer, with_next):
    m, d = h1.shape
    bm = PLE_BM
    row = lambda i: (i, 0)
    in_specs = [pl.BlockSpec((bm, d), row), pl.BlockSpec((bm, d), row),
                pl.BlockSpec((None, bm, D_PLE), lambda i: (layer, i, 0)),
                pl.BlockSpec((d, d), lambda i: (0, 0)),
                pl.BlockSpec((None, D_PLE, d), lambda i: (layer, 0, 0)),
                _layer_row(layer, d)]
    args = [h1, f, p, wg, wp, gains_ple]
    out_shape = [jax.ShapeDtypeStruct((m, d), F32)]
    out_specs = [pl.BlockSpec((bm, d), row)]
    blocks = [((bm, d), F32)] * 3 + [((bm, D_PLE), F32), ((d, d), BF16), ((D_PLE, d), BF16)]
    if with_next:
        in_specs.append(_layer_row(layer + 1, d))
        args.append(gains_mix)
        out_shape.append(jax.ShapeDtypeStruct((m, d), BF16))
        out_specs.append(pl.BlockSpec((bm, d), row))
        blocks.append(((bm, d), BF16))
    return pl.pallas_call(
        _ple_kernel,
        out_shape=tuple(out_shape),
        grid=(m // bm,),
        in_specs=in_specs,
        out_specs=tuple(out_specs),
        compiler_params=_params(("parallel",), _vmem_estimate(
            blocks, temporaries=[((bm, d), F32)])),
        name="ple",
    )(*args)


def kernel(x, p, norm_mix, w_in, q_norm, k_norm, forget_bias, gmlp_v_norm, gmlp_w_s, gmlp_b_s,
           pool_w, pool_scale, w_out, norm_ffn, w_ffn_gate, w_ffn_up, w_ffn_down,
           norm_ple, w_ple_gate, w_ple_proj):
    bsz, s, d = x.shape
    depth = w_in.shape[0]
    m = bsz * s
    w_in_t = jnp.swapaxes(w_in, 1, 2)
    w_f_t = jnp.pad(w_in_t[:, _F_LO:_F_HI, :], ((0, 0), (0, LANES - N_ATTN_HEADS), (0, 0)))
    w_in_casts = lambda layer: [(w_in_t, layer, 0, _F_LO), (w_in_t, layer, _F_HI, w_in_t.shape[1] - _F_HI)]
    wa16, wb16 = _cast_call(w_in_casts(0))
    f_bias = jnp.pad(forget_bias, ((0, 0), (0, LANES - N_ATTN_HEADS)))[:, None, :]
    b_rep = jnp.broadcast_to(gmlp_b_s[..., None], (depth, N_GMLP_HEADS, CHUNK, HEAD_DIM))
    v_gain = gmlp_v_norm[:, :, None, :]
    p_scale = pool_scale.reshape(depth, N_POOL_GROUPS, 1, HEAD_DIM)
    pool_w16 = pool_w.astype(BF16)
    w_pp16 = w_ple_proj.astype(BF16)
    later_weights = (w_out, w_ffn_gate, w_ffn_up, w_ffn_down, w_ple_gate)
    g_mix, g_ffn, g_ple = norm_mix[:, None, :], norm_ffn[:, None, :], norm_ple[:, None, :]
    g_q, g_k = q_norm[:, None, :], k_norm[:, None, :]
    p2 = p.reshape(depth, m, D_PLE)

    h = x.reshape(m, d)
    xn = _norm(h, g_mix, 0)
    for i in range(depth):
        with_next = i + 1 < depth
        proj = _in_proj(xn, wa16, wb16, g_q, g_k, i)
        proj3 = proj.reshape(bsz, s, D_MAIN)
        gq, gk = _fox_prep(xn.reshape(bsz, s, d), w_f_t, f_bias, i)
        casts = [(w, i, 0, w.shape[1]) for w in later_weights]
        if with_next:
            casts += w_in_casts(i + 1)
        ya, w16 = _attention(proj3, gq, gk, casts)
        w_out16, w_g16, w_u16, w_d16, w_pg16 = w16[:5]
        h1, xn_ffn = _mix_out(ya.reshape(m, ATTN_WIDTH), proj, h, w_out16, g_ffn,
                              v_gain, gmlp_w_s, b_rep, pool_w16, p_scale, i, s)
        f = _ffn(xn_ffn, w_g16, w_u16, w_d16)
        outs = _ple(h1, f, p2, w_pg16, w_pp16, g_ple, g_mix, i, with_next)
        h = outs[0]
        if with_next:
            xn = outs[1]
            wa16, wb16 = w16[5:]
    return h.reshape(bsz, s, d)
```
